```python
import math
import jax
import jax.numpy as jnp
from jax import lax
import numpy as np

D_MODEL = 4096
BATCH = 1
SEQ = 16384
DEPTH = 1
DEC_BATCH = 2
DEC_SEQ = 4096
PAST_LEN = 128

GRID_W = 64
Q_BLOCK = 128
ATT_WIDTH = D_MODEL // 2
HEAD_DIM = 128
N_Q_HEADS = ATT_WIDTH // HEAD_DIM
N_KV_HEADS = 4
Q_PER_KV = N_Q_HEADS // N_KV_HEADS
KV_WIDTH = N_KV_HEADS * HEAD_DIM
ROPE_SECTION = HEAD_DIM // 2
ROPE_THETA = 10000.0
ATT_SCALE = HEAD_DIM ** -0.5
SSM_WIDTH = D_MODEL // 4
SSM_GROUP = 16
N_SSM_GROUPS = SSM_WIDTH // SSM_GROUP
SSM_STATE = 64
DT_MIN = 1e-3
DT_MAX = 1e-1
N_MEM = 256
MEM_WIDTH = D_MODEL - ATT_WIDTH - SSM_WIDTH
N_MEM_HEADS = 4
MEM_HEAD_DIM = MEM_WIDTH // N_MEM_HEADS
MEM_SCALE = MEM_HEAD_DIM ** -0.5
IN_WIDTH = 2 * ATT_WIDTH + 2 * KV_WIDTH + 2 * SSM_WIDTH + 2 * MEM_WIDTH
SPLIT_POINTS = (ATT_WIDTH, ATT_WIDTH + KV_WIDTH, ATT_WIDTH + 2 * KV_WIDTH, 2 * ATT_WIDTH + 2 * KV_WIDTH, 2 * ATT_WIDTH + 2 * KV_WIDTH + SSM_WIDTH, 2 * ATT_WIDTH + 2 * KV_WIDTH + 2 * SSM_WIDTH, 2 * ATT_WIDTH + 2 * KV_WIDTH + 2 * SSM_WIDTH + MEM_WIDTH)
MIX_WIDTH = ATT_WIDTH + SSM_WIDTH + MEM_WIDTH
ALPHA = (2 * DEPTH) ** 0.25
BETA = (8 * DEPTH) ** -0.25
RMS_EPS = 1e-6
LN_EPS = 1e-5

kernel_name = 'hymba_gqa_s5_memxattn_deepnorm_encoder'


def _rms_heads(t, g):
    t = t.astype(jnp.float32)
    return t * lax.rsqrt(jnp.mean(t * t, axis=-1, keepdims=True) + RMS_EPS) * g.astype(jnp.float32)


def _axial_rope(L):
    rows = L // GRID_W
    row = jnp.broadcast_to(jnp.arange(rows, dtype=jnp.float32)[:, None], (rows, GRID_W)).reshape(L)
    col = jnp.broadcast_to(jnp.arange(GRID_W, dtype=jnp.float32)[None, :], (rows, GRID_W)).reshape(L)
    inv = ROPE_THETA ** (-jnp.arange(0, ROPE_SECTION, 2, dtype=jnp.float32) / ROPE_SECTION)
    ang_r = row[:, None] * inv[None, :]
    ang_c = col[:, None] * inv[None, :]
    ang = jnp.concatenate([ang_r, ang_r, ang_c, ang_c], axis=-1)
    return jnp.cos(ang)[:, None, :], jnp.sin(ang)[:, None, :]


def _apply_rope(t, cos, sin):
    sec = t.reshape(t.shape[:-1] + (2, 2, ROPE_SECTION // 2))
    rot = jnp.stack([-sec[..., 1, :], sec[..., 0, :]], axis=-2).reshape(t.shape)
    return t * cos + rot * sin


def _gqa_blocked(q, k, v):
    B, L = q.shape[0], q.shape[1]
    nb = L // Q_BLOCK
    qb = q.reshape(B, nb, Q_BLOCK, N_KV_HEADS, Q_PER_KV, HEAD_DIM).transpose(1, 0, 2, 3, 4, 5)

    def block(qblk):
        s = jnp.einsum('bqgrd,bkgd->bgrqk', qblk, k) * ATT_SCALE
        p = jax.nn.softmax(s, axis=-1)
        return jnp.einsum('bgrqk,bkgd->bqgrd', p, v)

    o = lax.map(block, qb)
    return o.transpose(1, 0, 2, 3, 4, 5).reshape(B, L, ATT_WIDTH)


def _s5_direction(u, lam_re, lam_im, log_step, b_re, b_im, reverse):
    f32 = jnp.float32
    lr = lam_re.astype(f32)
    li = lam_im.astype(f32)
    dt = jnp.exp(log_step.astype(f32))[:, None]
    mag = jnp.exp(lr * dt)
    ar = mag * jnp.cos(li * dt)
    ai = mag * jnp.sin(li * dt)
    den = lr * lr + li * li
    nr = ar - 1.0
    fr = (nr * lr + ai * li) / den
    fi = (ai * lr - nr * li) / den
    br = b_re.astype(f32)
    bi = b_im.astype(f32)
    bbar_r = fr[..., None] * br - fi[..., None] * bi
    bbar_i = fr[..., None] * bi + fi[..., None] * br
    bu_r = jnp.einsum('blgc,gpc->blgp', u, bbar_r)
    bu_i = jnp.einsum('blgc,gpc->blgp', u, bbar_i)
    a_r = jnp.broadcast_to(ar, bu_r.shape)
    a_i = jnp.broadcast_to(ai, bu_i.shape)

    def combine(e1, e2):
        a1r, a1i, b1r, b1i = e1
        a2r, a2i, b2r, b2i = e2
        return (a1r * a2r - a1i * a2i,
                a1r * a2i + a1i * a2r,
                a2r * b1r - a2i * b1i + b2r,
                a2r * b1i + a2i * b1r + b2i)

    _, _, xr, xi = lax.associative_scan(combine, (a_r, a_i, bu_r, bu_i), reverse=reverse, axis=1)
    return xr, xi


def _layer(x, mem, w_in, q_norm_g, k_norm_g, lam_re, lam_im, log_step, b_re, b_im,
           c_re, c_im, d_skip, w_glu, b_glu, w_mem_kv, w_out, ln_g, ln_b):
    f32 = jnp.float32
    B, L, _ = x.shape
    proj = jnp.einsum('bld,de->ble', x, w_in)
    q, k, v, g_att, u, g_ssm, q_mem, g_mem = jnp.split(proj, SPLIT_POINTS, axis=-1)

    cos, sin = _axial_rope(L)
    qh = _apply_rope(_rms_heads(q.reshape(B, L, N_Q_HEADS, HEAD_DIM), q_norm_g), cos, sin)
    kh = _apply_rope(_rms_heads(k.reshape(B, L, N_KV_HEADS, HEAD_DIM), k_norm_g), cos, sin)
    vh = v.reshape(B, L, N_KV_HEADS, HEAD_DIM).astype(f32)
    att = _gqa_blocked(qh, kh, vh) * jax.nn.silu(g_att.astype(f32))

    us = u.astype(f32).reshape(B, L, N_SSM_GROUPS, SSM_GROUP)
    xf_r, xf_i = _s5_direction(us, lam_re[0], lam_im[0], log_step[0], b_re[0], b_im[0], False)
    xb_r, xb_i = _s5_direction(us, lam_re[1], lam_im[1], log_step[1], b_re[1], b_im[1], True)
    y = (jnp.einsum('blgp,gcp->blgc', xf_r + xb_r, c_re.astype(f32))
         - jnp.einsum('blgp,gcp->blgc', xf_i + xb_i, c_im.astype(f32))
         + d_skip.astype(f32).reshape(N_SSM_GROUPS, SSM_GROUP) * us)
    y = jax.nn.gelu(y).reshape(B, L, SSM_WIDTH)
    ssm = y * jax.nn.sigmoid(y @ w_glu.astype(f32) + b_glu.astype(f32)) * jax.nn.silu(g_ssm.astype(f32))

    km, vm = jnp.split(jnp.einsum('bmd,de->bme', mem, w_mem_kv), 2, axis=-1)
    qm = q_mem.reshape(B, L, N_MEM_HEADS, MEM_HEAD_DIM).astype(f32)
    km = km.reshape(B, -1, N_MEM_HEADS, MEM_HEAD_DIM).astype(f32)
    vm = vm.reshape(B, -1, N_MEM_HEADS, MEM_HEAD_DIM).astype(f32)
    pm = jax.nn.softmax(jnp.einsum('blhd,bmhd->bhlm', qm, km) * MEM_SCALE, axis=-1)
    mem_out = jnp.einsum('bhlm,bmhd->blhd', pm, vm).reshape(B, L, MEM_WIDTH) * jax.nn.silu(g_mem.astype(f32))

    mixed = jnp.concatenate([att, ssm, mem_out], axis=-1).astype(x.dtype)
    h = ALPHA * x.astype(f32) + jnp.einsum('ble,ed->bld', mixed, w_out).astype(f32)
    mu = jnp.mean(h, axis=-1, keepdims=True)
    hc = h - mu
    var = jnp.mean(hc * hc, axis=-1, keepdims=True)
    out = hc * lax.rsqrt(var + LN_EPS) * ln_g.astype(f32) + ln_b.astype(f32)
    return out.astype(x.dtype)


def setup_inputs(seed: int = 0) -> dict:
    key = jax.random.key(seed)
    ks = jax.random.split(key, 24)
    f32 = jnp.float32
    nrm = lambda k, shape: jax.random.normal(k, shape, dtype=f32)
    x_prompt = nrm(ks[0], (BATCH, SEQ, D_MODEL))
    x_sample = nrm(ks[1], (DEC_BATCH, DEC_SEQ, D_MODEL))
    mem_prompt = nrm(ks[2], (BATCH, N_MEM, D_MODEL))
    mem_sample = nrm(ks[3], (DEC_BATCH, N_MEM, D_MODEL))
    w_in = nrm(ks[4], (DEPTH, D_MODEL, IN_WIDTH)) * D_MODEL ** -0.5
    q_norm_g = 1.0 + 0.02 * nrm(ks[5], (DEPTH, HEAD_DIM))
    k_norm_g = 1.0 + 0.02 * nrm(ks[6], (DEPTH, HEAD_DIM))
    ssm_lam_re = -0.5 + 0.01 * nrm(ks[7], (DEPTH, 2, N_SSM_GROUPS, SSM_STATE))
    ssm_lam_im = (math.pi * jnp.arange(SSM_STATE, dtype=f32)) + 0.01 * nrm(ks[8], (DEPTH, 2, N_SSM_GROUPS, SSM_STATE))
    ssm_log_step = jax.random.uniform(ks[9], (DEPTH, 2, N_SSM_GROUPS), dtype=f32, minval=math.log(DT_MIN), maxval=math.log(DT_MAX))
    ssm_b_re = nrm(ks[10], (DEPTH, 2, N_SSM_GROUPS, SSM_STATE, SSM_GROUP)) * (2 * SSM_GROUP) ** -0.5
    ssm_b_im = nrm(ks[11], (DEPTH, 2, N_SSM_GROUPS, SSM_STATE, SSM_GROUP)) * (2 * SSM_GROUP) ** -0.5
    ssm_c_re = nrm(ks[12], (DEPTH, N_SSM_GROUPS, SSM_GROUP, SSM_STATE)) * (2 * SSM_STATE) ** -0.5
    ssm_c_im = nrm(ks[13], (DEPTH, N_SSM_GROUPS, SSM_GROUP, SSM_STATE)) * (2 * SSM_STATE) ** -0.5
    ssm_d = nrm(ks[14], (DEPTH, SSM_WIDTH))
    w_glu = nrm(ks[15], (DEPTH, SSM_WIDTH, SSM_WIDTH)) * SSM_WIDTH ** -0.5
    b_glu = 0.01 * nrm(ks[16], (DEPTH, SSM_WIDTH))
    w_mem_kv = nrm(ks[17], (DEPTH, D_MODEL, 2 * MEM_WIDTH)) * D_MODEL ** -0.5
    w_out = nrm(ks[18], (DEPTH, MIX_WIDTH, D_MODEL)) * (MIX_WIDTH ** -0.5) * BETA
    ln_g = 1.0 + 0.02 * nrm(ks[19], (DEPTH, D_MODEL))
    ln_b = 0.02 * nrm(ks[20], (DEPTH, D_MODEL))
    return {'x_prompt': x_prompt, 'x_sample': x_sample, 'mem_prompt': mem_prompt, 'mem_sample': mem_sample,
            'w_in': w_in, 'q_norm_g': q_norm_g, 'k_norm_g': k_norm_g,
            'ssm_lam_re': ssm_lam_re, 'ssm_lam_im': ssm_lam_im, 'ssm_log_step': ssm_log_step,
            'ssm_b_re': ssm_b_re, 'ssm_b_im': ssm_b_im, 'ssm_c_re': ssm_c_re, 'ssm_c_im': ssm_c_im,
            'ssm_d': ssm_d, 'w_glu': w_glu, 'b_glu': b_glu, 'w_mem_kv': w_mem_kv, 'w_out': w_out,
            'ln_g': ln_g, 'ln_b': ln_b}


def reference(x_prompt, x_sample, mem_prompt, mem_sample, w_in, q_norm_g, k_norm_g,
              ssm_lam_re, ssm_lam_im, ssm_log_step, ssm_b_re, ssm_b_im, ssm_c_re, ssm_c_im,
              ssm_d, w_glu, b_glu, w_mem_kv, w_out, ln_g, ln_b):
    y_prompt = x_prompt
    y_sample = x_sample
    for l in range(DEPTH):
        y_prompt = _layer(y_prompt, mem_prompt, w_in[l], q_norm_g[l], k_norm_g[l],
                          ssm_lam_re[l], ssm_lam_im[l], ssm_log_step[l], ssm_b_re[l], ssm_b_im[l],
                          ssm_c_re[l], ssm_c_im[l], ssm_d[l], w_glu[l], b_glu[l], w_mem_kv[l],
                          w_out[l], ln_g[l], ln_b[l])
        y_sample = _layer(y_sample, mem_sample, w_in[l], q_norm_g[l], k_norm_g[l],
                          ssm_lam_re[l], ssm_lam_im[l], ssm_log_step[l], ssm_b_re[l], ssm_b_im[l],
                          ssm_c_re[l], ssm_c_im[l], ssm_d[l], w_glu[l], b_glu[l], w_mem_kv[l],
                          w_out[l], ln_g[l], ln_b[l])
    return (y_prompt, y_sample)
```

```python
import functools
import math

import jax
import jax.numpy as jnp
from jax import lax
from jax.experimental import pallas as pl
from jax.experimental.pallas import tpu as pltpu

F32 = jnp.float32
BF16 = jnp.bfloat16

HEAD_DIM = 128
N_KV_HEADS = 4
GRID_W = 64
ROPE_SECTION = HEAD_DIM // 2
ROPE_THETA = 10000.0
RMS_EPS = 1e-6
LN_EPS = 1e-5
SSM_GROUP = 16
SSM_STATE = 64
N_MEM_HEADS = 4
LOG2E = 1.4426950408889634

V7X_LANES = 128
V7X_VMEM_BYTES = 64 * 1024 * 1024
VMEM_LIMIT = V7X_VMEM_BYTES - 6 * 1024 * 1024

S5_CHUNK = 16
S5_UNIT = V7X_LANES
S5_UNIT_STATES = (S5_UNIT // SSM_GROUP) * SSM_STATE
S5_ROW_BLOCK = 256


def _params(sem):
    return pltpu.CompilerParams(dimension_semantics=sem, vmem_limit_bytes=VMEM_LIMIT)


def _silu(x):
    return x * jax.nn.sigmoid(x)


def _in_proj_kernel(x_ref, w_ref, cq_ref, saq_ref, sbq_ref, ck_ref, sak_ref, sbk_ref,
                    o_ref, u_ref, xb_ref, *, n_q, n_qk, n_plain, n_main):
    n = pl.program_id(1)

    @pl.when(n == 0)
    def _():
        xb_ref[...] = x_ref[...].astype(BF16)

    acc = jnp.dot(xb_ref[...], w_ref[...], preferred_element_type=F32)
    heads = acc.shape[1] // HEAD_DIM

    def norm_rope(cos_ref, sa_ref, sb_ref):
        for h in range(heads):
            t = acc[:, h * HEAD_DIM:(h + 1) * HEAD_DIM]
            r = lax.rsqrt(jnp.mean(t * t, axis=-1, keepdims=True) + RMS_EPS)
            y = (t * cos_ref[...]
                 + pltpu.roll(t, HEAD_DIM - ROPE_SECTION // 2, 1) * sa_ref[...]
                 + pltpu.roll(t, ROPE_SECTION // 2, 1) * sb_ref[...])
            o_ref[:, h * HEAD_DIM:(h + 1) * HEAD_DIM] = (y * r).astype(o_ref.dtype)

    @pl.when(n < n_q)
    def _():
        norm_rope(cq_ref, saq_ref, sbq_ref)

    @pl.when((n >= n_q) & (n < n_qk))
    def _():
        norm_rope(ck_ref, sak_ref, sbk_ref)

    @pl.when((n >= n_qk) & (n < n_plain))
    def _():
        o_ref[...] = acc.astype(o_ref.dtype)

    @pl.when((n >= n_plain) & (n < n_main))
    def _():
        o_ref[...] = _silu(acc).astype(o_ref.dtype)

    @pl.when(n >= n_main)
    def _():
        u_ref[...] = acc


def _in_proj(x, w, tables, seq_len, widths, bm, bn):
    T, D = x.shape
    att, kv, ssm, mem = widths
    n_q = att // bn
    n_qk = n_q + kv // bn
    n_plain = n_qk + (kv + mem) // bn
    n_main = n_plain + (att + ssm + mem) // bn
    n_u = ssm // bn
    w_main = n_main * bn
    pos_blocks = seq_len // bm
    tab_spec = pl.BlockSpec((bm, HEAD_DIM), lambda m, n: (m % pos_blocks, 0))
    kern = functools.partial(_in_proj_kernel, n_q=n_q, n_qk=n_qk, n_plain=n_plain, n_main=n_main)
    return pl.pallas_call(
        kern,
        grid=(T // bm, n_main + n_u),
        in_specs=[pl.BlockSpec((bm, D), lambda m, n: (m, 0)),
                  pl.BlockSpec((D, bn), lambda m, n: (0, n))] + [tab_spec] * 6,
        out_specs=[pl.BlockSpec((bm, bn), lambda m, n: (m, jnp.minimum(n, n_main - 1))),
                   pl.BlockSpec((bm, bn), lambda m, n: (m, jnp.clip(n - n_main, 0, n_u - 1)))],
        out_shape=[jax.ShapeDtypeStruct((T, w_main), BF16),
                   jax.ShapeDtypeStruct((T, ssm), F32)],
        scratch_shapes=[pltpu.VMEM((bm, D), BF16)],
        compiler_params=_params(("parallel", "arbitrary")),
    )(x, w, *tables)


def _attn_kernel(q_ref, k_ref, v_ref, g_ref, o_ref, *, bk):
    bq = q_ref.shape[0]
    n_rep = q_ref.shape[1] // HEAD_DIM
    nk = k_ref.shape[0] // bk
    for h in range(n_rep):
        sl = slice(h * HEAD_DIM, (h + 1) * HEAD_DIM)
        q = q_ref[:, sl]

        def body(j, carry, q=q):
            m, l, acc = carry
            off = pl.multiple_of(j * bk, bk)
            k = k_ref[pl.ds(off, bk), :]
            v = v_ref[pl.ds(off, bk), :]
            s = lax.dot_general(q, k, (((1,), (1,)), ((), ())), preferred_element_type=F32)
            m_new = jnp.maximum(m, jnp.max(s, axis=-1, keepdims=True))
            alpha = jnp.exp2(m - m_new)
            p = jnp.exp2(s - m_new)
            l = alpha * l + jnp.sum(p, axis=-1, keepdims=True)
            acc = alpha * acc + jnp.dot(p.astype(BF16), v, preferred_element_type=F32)
            return m_new, l, acc

        init = (jnp.full((bq, 1), -1e30, F32), jnp.zeros((bq, 1), F32), jnp.zeros((bq, HEAD_DIM), F32))
        _, l, acc = lax.fori_loop(0, nk, body, init)
        o_ref[:, sl] = (acc / l * g_ref[:, sl].astype(F32)).astype(o_ref.dtype)


def _attention(main, batch, seq_len, widths, bq, bk):
    att, kv, ssm, mem = widths
    T = main.shape[0]
    n_rep = att // kv
    wq = n_rep * HEAD_DIM
    k_col = att // HEAD_DIM
    v_col = (att + kv) // HEAD_DIM
    g_col = (att + 2 * kv + mem) // wq
    qb = seq_len // bq
    return pl.pallas_call(
        functools.partial(_attn_kernel, bk=bk),
        grid=(batch, N_KV_HEADS, qb),
        in_specs=[pl.BlockSpec((bq, wq), lambda b, g, i: (b * qb + i, g)),
                  pl.BlockSpec((seq_len, HEAD_DIM), lambda b, g, i: (b, k_col + g)),
                  pl.BlockSpec((seq_len, HEAD_DIM), lambda b, g, i: (b, v_col + g)),
                  pl.BlockSpec((bq, wq), lambda b, g, i: (b * qb + i, g_col + g))],
        out_specs=pl.BlockSpec((bq, wq), lambda b, g, i: (b * qb + i, g)),
        out_shape=jax.ShapeDtypeStruct((T, att), BF16),
        compiler_params=_params(("parallel", "parallel", "arbitrary")),
    )(main, main, main, main)


def _matmul_kernel(x_ref, w_ref, o_ref):
    o_ref[...] = jnp.dot(x_ref[...].astype(BF16), w_ref[...],
                         preferred_element_type=F32).astype(o_ref.dtype)


def _matmul(x, w, bn):
    M, K = x.shape
    N = w.shape[1]
    return pl.pallas_call(
        _matmul_kernel,
        grid=(N // bn,),
        in_specs=[pl.BlockSpec((M, K), lambda n: (0, 0)),
                  pl.BlockSpec((K, bn), lambda n: (0, n))],
        out_specs=pl.BlockSpec((M, bn), lambda n: (0, n)),
        out_shape=jax.ShapeDtypeStruct((M, N), BF16),
        compiler_params=_params(("arbitrary",)),
    )(x, w)


def _mem_attn_kernel(q_ref, k_ref, v_ref, g_ref, o_ref, *, scale):
    hd = q_ref.shape[1] // N_MEM_HEADS
    for h in range(N_MEM_HEADS):
        sl = slice(h * hd, (h + 1) * hd)
        q = (q_ref[:, sl].astype(F32) * scale).astype(BF16)
        s = lax.dot_general(q, k_ref[:, sl], (((1,), (1,)), ((), ())), preferred_element_type=F32)
        p = jnp.exp2(s - jnp.max(s, axis=-1, keepdims=True))
        l = jnp.sum(p, axis=-1, keepdims=True)
        o = jnp.dot(p.astype(BF16), v_ref[:, sl], preferred_element_type=F32)
        o_ref[:, sl] = (o / l * g_ref[:, sl].astype(F32)).astype(o_ref.dtype)


def _mem_attention(main, memkv, batch, seq_len, n_mem, widths, bl):
    att, kv, ssm, mem = widths
    T = main.shape[0]
    q_col = (att + 2 * kv) // mem
    g_col = (att + 2 * kv + mem + att + ssm) // mem
    lb = seq_len // bl
    scale = (mem // N_MEM_HEADS) ** -0.5 * LOG2E
    return pl.pallas_call(
        functools.partial(_mem_attn_kernel, scale=scale),
        grid=(batch, lb),
        in_specs=[pl.BlockSpec((bl, mem), lambda b, i: (b * lb + i, q_col)),
                  pl.BlockSpec((n_mem, mem), lambda b, i: (b, 0)),
                  pl.BlockSpec((n_mem, mem), lambda b, i: (b, 1)),
                  pl.BlockSpec((bl, mem), lambda b, i: (b * lb + i, g_col))],
        out_specs=pl.BlockSpec((bl, mem), lambda b, i: (b * lb + i, 0)),
        out_shape=jax.ShapeDtypeStruct((T, mem), BF16),
        compiler_params=_params(("parallel", "arbitrary")),
    )(main, memkv, memkv, main)


def _gelu_tanh(y):
    return 0.5 * y * (1.0 + jnp.tanh(math.sqrt(2.0 / math.pi) * (y + 0.044715 * (y * y * y))))


def _s5_kernel(*refs, reverse, tc):
    if reverse:
        u_ref, yin_ref, bm_ref, cm_ref, a_ref, y_ref, uc_ref, s_ref, st_ref = refs
    else:
        u_ref, tz_ref, bm_ref, cm_ref, a_ref, y_ref, uc_ref, s_ref, st_ref = refs
    rb = uc_ref.shape[0]
    ns = S5_UNIT_STATES

    @pl.when(pl.program_id(2) == 0)
    def _():
        st_ref[...] = jnp.zeros_like(st_ref)

    for j in range(tc):
        uc_ref[:, j * S5_UNIT:(j + 1) * S5_UNIT] = u_ref[pl.ds(j, rb, stride=tc), :].astype(BF16)
    uc = uc_ref[...]
    s_ref[...] = jnp.dot(uc, bm_ref[...], preferred_element_type=F32)
    ar = a_ref[0:1, :]
    ai = a_ref[1:2, :]

    def body(i, carry):
        xr, xi = carry
        c = (rb - 1 - i) if reverse else i
        sr = s_ref[pl.ds(c, 1), 0:ns]
        si = s_ref[pl.ds(c, 1), ns:2 * ns]
        s_ref[pl.ds(c, 1), 0:ns] = xr
        s_ref[pl.ds(c, 1), ns:2 * ns] = xi
        return ar * xr - ai * xi + sr, ar * xi + ai * xr + si

    xr, xi = lax.fori_loop(0, rb, body, (st_ref[0:1, :], st_ref[1:2, :]), unroll=8)
    st_ref[0:1, :] = xr
    st_ref[1:2, :] = xi

    y = jnp.dot(s_ref[...].astype(BF16), cm_ref[...], preferred_element_type=F32)
    if not reverse:
        y = y + jnp.dot(uc, tz_ref[...], preferred_element_type=F32)
    for t in range(tc):
        rows = pl.ds(t, rb, stride=tc)
        yt = y[:, t * S5_UNIT:(t + 1) * S5_UNIT]
        if reverse:
            y_ref[rows, :] = _gelu_tanh(yin_ref[rows, :] + yt)
        else:
            y_ref[rows, :] = yt


def _s5_pass(u, y_in, ops, batch, seq_len, reverse, rb):
    T, ssm = u.shape
    tc = S5_CHUNK
    n_units = ssm // S5_UNIT
    tok = rb * tc
    nb = seq_len // tok
    ns = S5_UNIT_STATES

    def row(b, r):
        return b * nb + ((nb - 1 - r) if reverse else r)

    tok_spec = pl.BlockSpec((tok, S5_UNIT), lambda n, b, r: (row(b, r), n))
    wspec = lambda shape: pl.BlockSpec((None,) + shape, lambda n, b, r: (n, 0, 0))
    if reverse:
        tz, bm, cm, a = None, ops["bmb"], ops["cmb"], ops["ab"]
        ins = [u, y_in, bm, cm, a]
        in_specs = [tok_spec, tok_spec]
    else:
        tz, bm, cm, a = ops["tz"], ops["bmf"], ops["cmf"], ops["af"]
        ins = [u, tz, bm, cm, a]
        in_specs = [tok_spec, wspec((tc * S5_UNIT, tc * S5_UNIT))]
    in_specs += [wspec((tc * S5_UNIT, 2 * ns)), wspec((2 * ns, tc * S5_UNIT)), wspec((2, ns))]
    return pl.pallas_call(
        functools.partial(_s5_kernel, reverse=reverse, tc=tc),
        grid=(n_units, batch, nb),
        in_specs=in_specs,
        out_specs=tok_spec,
        out_shape=jax.ShapeDtypeStruct((T, ssm), F32),
        scratch_shapes=[pltpu.VMEM((rb, tc * S5_UNIT), BF16),
                        pltpu.VMEM((rb, 2 * ns), F32),
                        pltpu.VMEM((2, ns), F32)],
        compiler_params=_params(("parallel", "arbitrary", "arbitrary")),
    )(*ins)


def _s5_operators(lam_re, lam_im, log_step, b_re, b_im, c_re, c_im, d_skip):
    hi = lax.Precision.HIGHEST
    tc = S5_CHUNK
    G = lam_re.shape[1]
    P, C = SSM_STATE, SSM_GROUP
    gu = S5_UNIT // C
    nu = G // gu
    lr = lam_re.astype(F32)
    li = lam_im.astype(F32)
    dt = jnp.exp(log_step.astype(F32))[..., None]
    mag = jnp.exp(lr * dt)
    ar = mag * jnp.cos(li * dt)
    ai = mag * jnp.sin(li * dt)
    den = lr * lr + li * li
    nr = ar - 1.0
    fr = (nr * lr + ai * li) / den
    fi = (ai * lr - nr * li) / den
    br = b_re.astype(F32)
    bi = b_im.astype(F32)
    bbar_r = fr[..., None] * br - fi[..., None] * bi
    bbar_i = fr[..., None] * bi + fi[..., None] * br
    e = jnp.arange(tc + 1, dtype=F32)
    pmag = jnp.exp(lr[..., None] * dt[..., None] * e)
    pang = li[..., None] * dt[..., None] * e
    pr = pmag * jnp.cos(pang)
    pi = pmag * jnp.sin(pang)
    cr = c_re.astype(F32)
    ci = c_im.astype(F32)
    eye = jnp.eye(gu, dtype=F32)

    ops = {}
    def lag_kernels(d):
        car = jnp.einsum('gcp,gpt->tgcp', cr, pr[d]) - jnp.einsum('gcp,gpt->tgcp', ci, pi[d])
        cai = jnp.einsum('gcp,gpt->tgcp', cr, pi[d]) + jnp.einsum('gcp,gpt->tgcp', ci, pr[d])
        k = (jnp.einsum('tgcp,gpd->tgcd', car, bbar_r[d], precision=hi)
             - jnp.einsum('tgcp,gpd->tgcd', cai, bbar_i[d], precision=hi))
        return k[:tc]
    kf = lag_kernels(0)
    kb = lag_kernels(1)
    dmat = d_skip.astype(F32).reshape(G, C)[:, :, None] * jnp.eye(C, dtype=F32)[None]
    k0 = kf[0] + kb[0] + dmat
    kall = jnp.concatenate([kb[1:][::-1], k0[None], kf[1:]], axis=0)
    idx = (jnp.arange(tc)[None, :] - jnp.arange(tc)[:, None]) + (tc - 1)
    tz = kall[idx]
    tz = tz.reshape(tc, tc, nu, gu, C, C)
    tz = jnp.einsum('jtngcd,gh->njgdthc', tz, eye).reshape(nu, tc * S5_UNIT, tc * S5_UNIT)
    ops["tz"] = tz.astype(BF16)

    for d, name in ((0, "f"), (1, "b")):
        ex = (tc - 1 - jnp.arange(tc)) if d == 0 else jnp.arange(tc)
        wr = pr[d][..., ex]
        wi = pi[d][..., ex]
        sr = jnp.einsum('gpj,gpd->jgdp', wr, bbar_r[d]) - jnp.einsum('gpj,gpd->jgdp', wi, bbar_i[d])
        si = jnp.einsum('gpj,gpd->jgdp', wr, bbar_i[d]) + jnp.einsum('gpj,gpd->jgdp', wi, bbar_r[d])
        bm = jnp.stack([sr, si], axis=0).reshape(2, tc, nu, gu, C, P)
        bm = jnp.einsum('qjngdp,gh->njgdqhp', bm, eye).reshape(nu, tc * S5_UNIT, 2 * gu * P)
        ops["bm" + name] = bm.astype(BF16)
        ex = (jnp.arange(tc) + 1) if d == 0 else (tc - jnp.arange(tc))
        er = pr[d][..., ex]
        ei = pi[d][..., ex]
        m_re = jnp.einsum('gcp,gpt->gptc', cr, er) - jnp.einsum('gcp,gpt->gptc', ci, ei)
        m_im = -(jnp.einsum('gcp,gpt->gptc', cr, ei) + jnp.einsum('gcp,gpt->gptc', ci, er))
        cm = jnp.stack([m_re, m_im], axis=0).reshape(2, nu, gu, P, tc, C)
        cm = jnp.einsum('qngptc,gh->nqgpthc', cm, eye).reshape(nu, 2 * gu * P, tc * S5_UNIT)
        ops["cm" + name] = cm.astype(BF16)
        a = jnp.stack([pr[d][..., tc], pi[d][..., tc]], axis=0)
        ops["a" + name] = a.reshape(2, nu, gu * P).transpose(1, 0, 2)
    return ops


def _glu_kernel(y_ref, w_ref, b_ref, g_ref, o_ref):
    y = y_ref[...]
    z = jnp.dot(y.astype(BF16), w_ref[...], preferred_element_type=F32) + b_ref[...]
    o_ref[...] = (y * jax.nn.sigmoid(z) * g_ref[...].astype(F32)).astype(o_ref.dtype)


def _glu(y, w, b, main, widths, bm):
    att, kv, ssm, mem = widths
    T = y.shape[0]
    g_col = (att + 2 * kv + mem + att) // ssm
    return pl.pallas_call(
        _glu_kernel,
        grid=(T // bm,),
        in_specs=[pl.BlockSpec((bm, ssm), lambda m: (m, 0)),
                  pl.BlockSpec((ssm, ssm), lambda m: (0, 0)),
                  pl.BlockSpec((1, ssm), lambda m: (0, 0)),
                  pl.BlockSpec((bm, ssm), lambda m: (m, g_col))],
        out_specs=pl.BlockSpec((bm, ssm), lambda m: (m, 0)),
        out_shape=jax.ShapeDtypeStruct((T, ssm), BF16),
        compiler_params=_params(("parallel",)),
    )(y, w, b, main)


def _out_kernel(att_ref, ssm_ref, mem_ref, wa_ref, ws_ref, wm_ref, x_ref, g_ref, b_ref,
                o_ref, h_ref, *, alpha):
    n = pl.program_id(1)
    nn = pl.num_programs(1)
    h = (jnp.dot(att_ref[...], wa_ref[...], preferred_element_type=F32)
         + jnp.dot(ssm_ref[...], ws_ref[...], preferred_element_type=F32)
         + jnp.dot(mem_ref[...], wm_ref[...], preferred_element_type=F32))
    h_ref[n] = alpha * x_ref[...] + h

    @pl.when(n == nn - 1)
    def _():
        nt, bm, bn = h_ref.shape
        d = nt * bn
        tot = jnp.zeros((bm, 1), F32)
        for j in range(nt):
            tot = tot + jnp.sum(h_ref[j], axis=-1, keepdims=True)
        mu = tot / d
        sq = jnp.zeros((bm, 1), F32)
        for j in range(nt):
            hc = h_ref[j] - mu
            sq = sq + jnp.sum(hc * hc, axis=-1, keepdims=True)
        r = lax.rsqrt(sq / d + LN_EPS)
        for j in range(nt):
            sl = slice(j * bn, (j + 1) * bn)
            o_ref[:, sl] = (h_ref[j] - mu) * r * g_ref[:, sl] + b_ref[:, sl]


def _out_proj(att_o, ssm_o, mem_o, w_out, x, ln_g, ln_b, alpha, bm, bn):
    T, D = x.shape
    att, ssm, mem = att_o.shape[1], ssm_o.shape[1], mem_o.shape[1]
    return pl.pallas_call(
        functools.partial(_out_kernel, alpha=alpha),
        grid=(T // bm, D // bn),
        in_specs=[pl.BlockSpec((bm, att), lambda m, n: (m, 0)),
                  pl.BlockSpec((bm, ssm), lambda m, n: (m, 0)),
                  pl.BlockSpec((bm, mem), lambda m, n: (m, 0)),
                  pl.BlockSpec((att, bn), lambda m, n: (0, n)),
                  pl.BlockSpec((ssm, bn), lambda m, n: (att // ssm, n)),
                  pl.BlockSpec((mem, bn), lambda m, n: ((att + ssm) // mem, n)),
                  pl.BlockSpec((bm, bn), lambda m, n: (m, n)),
                  pl.BlockSpec((1, D), lambda m, n: (0, 0)),
                  pl.BlockSpec((1, D), lambda m, n: (0, 0))],
        out_specs=pl.BlockSpec((bm, D), lambda m, n: (m, 0)),
        out_shape=jax.ShapeDtypeStruct((T, D), F32),
        scratch_shapes=[pltpu.VMEM((D // bn, bm, bn), F32)],
        compiler_params=_params(("parallel", "arbitrary")),
    )(att_o, ssm_o, mem_o, w_out, w_out, w_out, x, ln_g, ln_b)


def _rope_tables(seq_len, gain, scale):
    rows = seq_len // GRID_W
    row = jnp.broadcast_to(jnp.arange(rows, dtype=F32)[:, None], (rows, GRID_W)).reshape(seq_len)
    col = jnp.broadcast_to(jnp.arange(GRID_W, dtype=F32)[None, :], (rows, GRID_W)).reshape(seq_len)
    inv = ROPE_THETA ** (-jnp.arange(0, ROPE_SECTION, 2, dtype=F32) / ROPE_SECTION)
    ang_r = row[:, None] * inv[None, :]
    ang_c = col[:, None] * inv[None, :]
    ang = jnp.concatenate([ang_r, ang_r, ang_c, ang_c], axis=-1)
    cos, sin = jnp.cos(ang), jnp.sin(ang)
    half = ROPE_SECTION // 2
    first = (jnp.arange(HEAD_DIM) % ROPE_SECTION) < half
    g = gain.astype(F32) * scale
    c_t = cos * g[None, :]
    sa_t = jnp.where(first[None, :], -sin * jnp.roll(g, -half)[None, :], 0.0)
    sb_t = jnp.where(first[None, :], 0.0, sin * jnp.roll(g, half)[None, :])
    return c_t, sa_t, sb_t


def _pick(n, pref):
    b = min(n, pref)
    while n % b:
        b //= 2
    return b


def _layer(x3, mem3, w_in_p, q_tabs_fn, widths, ops, w_glu, b_glu, w_mem_kv, w_out, ln_g, ln_b, alpha):
    B, L, D = x3.shape
    att, kv, ssm, mem = widths
    x = x3.reshape(B * L, D)
    n_mem = mem3.shape[1]
    tables = q_tabs_fn(L)
    bm = _pick(L, 512)
    main, u = _in_proj(x, w_in_p, tables, L, widths, bm, 512)
    att_o = _attention(main, B, L, widths, _pick(L, 256), _pick(L, 512))
    memkv = _matmul(mem3.reshape(B * n_mem, D), w_mem_kv, 512)
    mem_o = _mem_attention(main, memkv, B, L, n_mem, widths, _pick(L, 512))
    rb = _pick(L // S5_CHUNK, S5_ROW_BLOCK)
    y_f = _s5_pass(u, None, ops, B, L, False, rb)
    y = _s5_pass(u, y_f, ops, B, L, True, rb)
    ssm_o = _glu(y, w_glu, b_glu, main, widths, _pick(B * L, 512))
    out = _out_proj(att_o, ssm_o, mem_o, w_out, x, ln_g, ln_b, alpha, _pick(B * L, 512), 512)
    return out.reshape(B, L, D)


def kernel(x_prompt, x_sample, mem_prompt, mem_sample, w_in, q_norm_g, k_norm_g, ssm_lam_re, ssm_lam_im, ssm_log_step, ssm_b_re, ssm_b_im, ssm_c_re, ssm_c_im, ssm_d, w_glu, b_glu, w_mem_kv, w_out, ln_g, ln_b):
    depth, d_model, in_width = w_in.shape
    att = d_model // 2
    kv = N_KV_HEADS * HEAD_DIM
    ssm = d_model // 4
    mem = d_model - att - ssm
    widths = (att, kv, ssm, mem)
    alpha = (2 * depth) ** 0.25
    o = [0, att, att + kv, att + 2 * kv, 2 * att + 2 * kv, 2 * att + 2 * kv + ssm,
         2 * att + 2 * kv + 2 * ssm, 2 * att + 2 * kv + 2 * ssm + mem, in_width]
    seg = lambda i: (o[i], o[i + 1])
    order = [seg(0), seg(1), seg(2), seg(6), seg(3), seg(5), seg(7), seg(4)]

    y_p, y_s = x_prompt, x_sample
    for l in range(depth):
        w_in_p = jnp.concatenate([w_in[l][:, a:b] for a, b in order], axis=1).astype(BF16)
        att_scale = HEAD_DIM ** -0.5 * LOG2E

        def tabs(L, l=l):
            return (_rope_tables(L, q_norm_g[l], att_scale) + _rope_tables(L, k_norm_g[l], 1.0))

        ops = _s5_operators(ssm_lam_re[l], ssm_lam_im[l], ssm_log_step[l], ssm_b_re[l], ssm_b_im[l],
                            ssm_c_re[l], ssm_c_im[l], ssm_d[l])
        args = (w_in_p, tabs, widths, ops, w_glu[l].astype(BF16), b_glu[l].astype(F32)[None, :],
                w_mem_kv[l].astype(BF16), w_out[l].astype(BF16),
                ln_g[l].astype(F32)[None, :], ln_b[l].astype(F32)[None, :], alpha)
        y_p = _layer(y_p, mem_prompt, *args)
        y_s = _layer(y_s, mem_sample, *args)
    return (y_p, y_s)
```

```python
import functools
import math

import jax
import jax.numpy as jnp
from jax import lax
from jax.experimental import pallas as pl
from jax.experimental.pallas import tpu as pltpu

F32 = jnp.float32
BF16 = jnp.bfloat16

HEAD_DIM = 128
N_KV_HEADS = 4
GRID_W = 64
ROPE_SECTION = HEAD_DIM // 2
ROPE_THETA = 10000.0
RMS_EPS = 1e-6
LN_EPS = 1e-5
SSM_GROUP = 16
SSM_STATE = 64
N_MEM_HEADS = 4
LOG2E = 1.4426950408889634

V7X_LANES = 128
V7X_VMEM_BYTES = 64 * 1024 * 1024
VMEM_LIMIT = V7X_VMEM_BYTES - 6 * 1024 * 1024

S5_CHUNK = 16
S5_UNIT = V7X_LANES
S5_UNIT_STATES = (S5_UNIT // SSM_GROUP) * SSM_STATE
S5_ROW_BLOCK = 256


def _params(sem):
    return pltpu.CompilerParams(dimension_semantics=sem, vmem_limit_bytes=VMEM_LIMIT)


def _silu(x):
    return x * jax.nn.sigmoid(x)


def _in_proj_kernel(x_ref, w_ref, cq_ref, saq_ref, sbq_ref, ck_ref, sak_ref, sbk_ref,
                    o_ref, u_ref, xb_ref, *, n_q, n_qk, n_plain, n_main):
    n = pl.program_id(1)

    @pl.when(n == 0)
    def _():
        xb_ref[...] = x_ref[...].astype(BF16)

    acc = jnp.dot(xb_ref[...], w_ref[...], preferred_element_type=F32)
    heads = acc.shape[1] // HEAD_DIM

    def norm_rope(cos_ref, sa_ref, sb_ref):
        for h in range(heads):
            t = acc[:, h * HEAD_DIM:(h + 1) * HEAD_DIM]
            r = lax.rsqrt(jnp.mean(t * t, axis=-1, keepdims=True) + RMS_EPS)
            y = (t * cos_ref[...]
                 + pltpu.roll(t, HEAD_DIM - ROPE_SECTION // 2, 1) * sa_ref[...]
                 + pltpu.roll(t, ROPE_SECTION // 2, 1) * sb_ref[...])
            o_ref[:, h * HEAD_DIM:(h + 1) * HEAD_DIM] = (y * r).astype(o_ref.dtype)

    @pl.when(n < n_q)
    def _():
        norm_rope(cq_ref, saq_ref, sbq_ref)

    @pl.when((n >= n_q) & (n < n_qk))
    def _():
        norm_rope(ck_ref, sak_ref, sbk_ref)

    @pl.when((n >= n_qk) & (n < n_plain))
    def _():
        o_ref[...] = acc.astype(o_ref.dtype)

    @pl.when((n >= n_plain) & (n < n_main))
    def _():
        o_ref[...] = _silu(acc).astype(o_ref.dtype)

    @pl.when(n >= n_main)
    def _():
        u_ref[...] = acc


def _in_proj(x, w, tables, seq_len, widths, bm, bn):
    T, D = x.shape
    att, kv, ssm, mem = widths
    n_q = att // bn
    n_qk = n_q + kv // bn
    n_plain = n_qk + (kv + mem) // bn
    n_main = n_plain + (att + ssm + mem) // bn
    n_u = ssm // bn
    w_main = n_main * bn
    pos_blocks = seq_len // bm
    tab_spec = pl.BlockSpec((bm, HEAD_DIM), lambda m, n: (m % pos_blocks, 0))
    kern = functools.partial(_in_proj_kernel, n_q=n_q, n_qk=n_qk, n_plain=n_plain, n_main=n_main)
    return pl.pallas_call(
        kern,
        grid=(T // bm, n_main + n_u),
        in_specs=[pl.BlockSpec((bm, D), lambda m, n: (m, 0)),
                  pl.BlockSpec((D, bn), lambda m, n: (0, n))] + [tab_spec] * 6,
        out_specs=[pl.BlockSpec((bm, bn), lambda m, n: (m, jnp.minimum(n, n_main - 1))),
                   pl.BlockSpec((bm, bn), lambda m, n: (m, jnp.clip(n - n_main, 0, n_u - 1)))],
        out_shape=[jax.ShapeDtypeStruct((T, w_main), BF16),
                   jax.ShapeDtypeStruct((T, ssm), F32)],
        scratch_shapes=[pltpu.VMEM((bm, D), BF16)],
        compiler_params=_params(("parallel", "arbitrary")),
    )(x, w, *tables)


def _attn_kernel(q_ref, k_ref, v_ref, g_ref, o_ref, kt_ref, vp_ref, acc_ref, m_ref, *, bk, unroll):
    bq = q_ref.shape[0]
    n_rep = q_ref.shape[1] // HEAD_DIM
    seq = k_ref.shape[0]
    nk = seq // bk

    @pl.when(pl.program_id(2) == 0)
    def _():
        for c in range(seq // bk):
            kt_ref[:, c * bk:(c + 1) * bk] = k_ref[c * bk:(c + 1) * bk, :].T
        vp_ref[:, :HEAD_DIM] = v_ref[...]
        vp_ref[:, HEAD_DIM:] = jnp.ones((seq, HEAD_DIM), vp_ref.dtype)

    acc_ref[...] = jnp.zeros_like(acc_ref)
    m_ref[...] = jnp.full(m_ref.shape, -1e30, F32)

    def body(j, carry):
        off = pl.multiple_of(j * bk, bk)
        kt = kt_ref[:, pl.ds(off, bk)]
        vp = vp_ref[pl.ds(off, bk), :]
        for h in range(n_rep):
            q = q_ref[:, h * HEAD_DIM:(h + 1) * HEAD_DIM]
            s = jnp.dot(q, kt, preferred_element_type=F32)
            m_old = m_ref[h]
            m_new = jnp.maximum(m_old, jnp.max(s, axis=-1, keepdims=True))
            p = jnp.exp2(s - jnp.concatenate([m_new] * (bk // HEAD_DIM), axis=1)).astype(BF16)
            alpha = jnp.exp2(m_old - m_new)
            acc_ref[h] = (jnp.concatenate([alpha, alpha], axis=1) * acc_ref[h]
                          + jnp.dot(p, vp, preferred_element_type=F32))
            m_ref[h] = m_new
        return carry

    lax.fori_loop(0, nk, body, 0, unroll=unroll)
    for h in range(n_rep):
        a = acc_ref[h]
        sl = slice(h * HEAD_DIM, (h + 1) * HEAD_DIM)
        o_ref[:, sl] = (a[:, :HEAD_DIM] / a[:, HEAD_DIM:] * g_ref[:, sl].astype(F32)).astype(o_ref.dtype)


def ATTN_BLOCKS(seq_len):
    bk = _pick(seq_len, 512)
    return _pick(seq_len, 512), bk, math.gcd(seq_len // bk, 4)


def _attention(main, batch, seq_len, widths, bq, bk, unroll):
    att, kv, ssm, mem = widths
    T = main.shape[0]
    n_rep = att // kv
    wq = n_rep * HEAD_DIM
    k_col = att // HEAD_DIM
    v_col = (att + kv) // HEAD_DIM
    g_col = (att + 2 * kv + mem) // wq
    qb = seq_len // bq
    return pl.pallas_call(
        functools.partial(_attn_kernel, bk=bk, unroll=unroll),
        grid=(batch, N_KV_HEADS, qb),
        in_specs=[pl.BlockSpec((bq, wq), lambda b, g, i: (b * qb + i, g)),
                  pl.BlockSpec((seq_len, HEAD_DIM), lambda b, g, i: (b, k_col + g)),
                  pl.BlockSpec((seq_len, HEAD_DIM), lambda b, g, i: (b, v_col + g)),
                  pl.BlockSpec((bq, wq), lambda b, g, i: (b * qb + i, g_col + g))],
        out_specs=pl.BlockSpec((bq, wq), lambda b, g, i: (b * qb + i, g)),
        out_shape=jax.ShapeDtypeStruct((T, att), BF16),
        scratch_shapes=[pltpu.VMEM((HEAD_DIM, seq_len), BF16),
                        pltpu.VMEM((seq_len, 2 * HEAD_DIM), BF16),
                        pltpu.VMEM((n_rep, bq, 2 * HEAD_DIM), F32),
                        pltpu.VMEM((n_rep, bq, HEAD_DIM), F32)],
        compiler_params=_params(("parallel", "parallel", "arbitrary")),
        name="gqa_attention",
    )(main, main, main, main)


def _matmul_kernel(x_ref, w_ref, o_ref):
    o_ref[...] = jnp.dot(x_ref[...].astype(BF16), w_ref[...],
                         preferred_element_type=F32).astype(o_ref.dtype)


def _matmul(x, w, bn):
    M, K = x.shape
    N = w.shape[1]
    return pl.pallas_call(
        _matmul_kernel,
        grid=(N // bn,),
        in_specs=[pl.BlockSpec((M, K), lambda n: (0, 0)),
                  pl.BlockSpec((K, bn), lambda n: (0, n))],
        out_specs=pl.BlockSpec((M, bn), lambda n: (0, n)),
        out_shape=jax.ShapeDtypeStruct((M, N), BF16),
        compiler_params=_params(("arbitrary",)),
    )(x, w)


def _mem_attn_kernel(q_ref, k_ref, v_ref, g_ref, o_ref, *, scale):
    hd = q_ref.shape[1] // N_MEM_HEADS
    for h in range(N_MEM_HEADS):
        sl = slice(h * hd, (h + 1) * hd)
        q = (q_ref[:, sl].astype(F32) * scale).astype(BF16)
        s = lax.dot_general(q, k_ref[:, sl], (((1,), (1,)), ((), ())), preferred_element_type=F32)
        p = jnp.exp2(s - jnp.max(s, axis=-1, keepdims=True))
        l = jnp.sum(p, axis=-1, keepdims=True)
        o = jnp.dot(p.astype(BF16), v_ref[:, sl], preferred_element_type=F32)
        o_ref[:, sl] = (o / l * g_ref[:, sl].astype(F32)).astype(o_ref.dtype)


def _mem_attention(main, memkv, batch, seq_len, n_mem, widths, bl):
    att, kv, ssm, mem = widths
    T = main.shape[0]
    q_col = (att + 2 * kv) // mem
    g_col = (att + 2 * kv + mem + att + ssm) // mem
    lb = seq_len // bl
    scale = (mem // N_MEM_HEADS) ** -0.5 * LOG2E
    return pl.pallas_call(
        functools.partial(_mem_attn_kernel, scale=scale),
        grid=(batch, lb),
        in_specs=[pl.BlockSpec((bl, mem), lambda b, i: (b * lb + i, q_col)),
                  pl.BlockSpec((n_mem, mem), lambda b, i: (b, 0)),
                  pl.BlockSpec((n_mem, mem), lambda b, i: (b, 1)),
                  pl.BlockSpec((bl, mem), lambda b, i: (b * lb + i, g_col))],
        out_specs=pl.BlockSpec((bl, mem), lambda b, i: (b * lb + i, 0)),
        out_shape=jax.ShapeDtypeStruct((T, mem), BF16),
        compiler_params=_params(("parallel", "arbitrary")),
    )(main, memkv, memkv, main)


def _gelu_tanh(y):
    return 0.5 * y * (1.0 + jnp.tanh(math.sqrt(2.0 / math.pi) * (y + 0.044715 * (y * y * y))))


def _s5_kernel(*refs, reverse, tc):
    if reverse:
        u_ref, yin_ref, bm_ref, cm_ref, a_ref, y_ref, uc_ref, s_ref, st_ref = refs
    else:
        u_ref, tz_ref, bm_ref, cm_ref, a_ref, y_ref, uc_ref, s_ref, st_ref = refs
    rb = uc_ref.shape[0]
    ns = S5_UNIT_STATES

    @pl.when(pl.program_id(2) == 0)
    def _():
        st_ref[...] = jnp.zeros_like(st_ref)

    for j in range(tc):
        uc_ref[:, j * S5_UNIT:(j + 1) * S5_UNIT] = u_ref[pl.ds(j, rb, stride=tc), :].astype(BF16)
    uc = uc_ref[...]
    s_ref[...] = jnp.dot(uc, bm_ref[...], preferred_element_type=F32)
    ar = a_ref[0:1, :]
    ai = a_ref[1:2, :]

    def body(i, carry):
        xr, xi = carry
        c = (rb - 1 - i) if reverse else i
        sr = s_ref[pl.ds(c, 1), 0:ns]
        si = s_ref[pl.ds(c, 1), ns:2 * ns]
        s_ref[pl.ds(c, 1), 0:ns] = xr
        s_ref[pl.ds(c, 1), ns:2 * ns] = xi
        return ar * xr - ai * xi + sr, ar * xi + ai * xr + si

    xr, xi = lax.fori_loop(0, rb, body, (st_ref[0:1, :], st_ref[1:2, :]), unroll=8)
    st_ref[0:1, :] = xr
    st_ref[1:2, :] = xi

    y = jnp.dot(s_ref[...].astype(BF16), cm_ref[...], preferred_element_type=F32)
    if not reverse:
        y = y + jnp.dot(uc, tz_ref[...], preferred_element_type=F32)
    for t in range(tc):
        rows = pl.ds(t, rb, stride=tc)
        yt = y[:, t * S5_UNIT:(t + 1) * S5_UNIT]
        if reverse:
            y_ref[rows, :] = _gelu_tanh(yin_ref[rows, :] + yt)
        else:
            y_ref[rows, :] = yt


def _s5_pass(u, y_in, ops, batch, seq_len, reverse, rb):
    T, ssm = u.shape
    tc = S5_CHUNK
    n_units = ssm // S5_UNIT
    tok = rb * tc
    nb = seq_len // tok
    ns = S5_UNIT_STATES

    def row(b, r):
        return b * nb + ((nb - 1 - r) if reverse else r)

    tok_spec = pl.BlockSpec((tok, S5_UNIT), lambda n, b, r: (row(b, r), n))
    wspec = lambda shape: pl.BlockSpec((None,) + shape, lambda n, b, r: (n, 0, 0))
    if reverse:
        tz, bm, cm, a = None, ops["bmb"], ops["cmb"], ops["ab"]
        ins = [u, y_in, bm, cm, a]
        in_specs = [tok_spec, tok_spec]
    else:
        tz, bm, cm, a = ops["tz"], ops["bmf"], ops["cmf"], ops["af"]
        ins = [u, tz, bm, cm, a]
        in_specs = [tok_spec, wspec((tc * S5_UNIT, tc * S5_UNIT))]
    in_specs += [wspec((tc * S5_UNIT, 2 * ns)), wspec((2 * ns, tc * S5_UNIT)), wspec((2, ns))]
    return pl.pallas_call(
        functools.partial(_s5_kernel, reverse=reverse, tc=tc),
        grid=(n_units, batch, nb),
        in_specs=in_specs,
        out_specs=tok_spec,
        out_shape=jax.ShapeDtypeStruct((T, ssm), F32),
        scratch_shapes=[pltpu.VMEM((rb, tc * S5_UNIT), BF16),
                        pltpu.VMEM((rb, 2 * ns), F32),
                        pltpu.VMEM((2, ns), F32)],
        compiler_params=_params(("parallel", "arbitrary", "arbitrary")),
    )(*ins)


def _s5_operators(lam_re, lam_im, log_step, b_re, b_im, c_re, c_im, d_skip):
    hi = lax.Precision.HIGHEST
    tc = S5_CHUNK
    G = lam_re.shape[1]
    P, C = SSM_STATE, SSM_GROUP
    gu = S5_UNIT // C
    nu = G // gu
    lr = lam_re.astype(F32)
    li = lam_im.astype(F32)
    dt = jnp.exp(log_step.astype(F32))[..., None]
    mag = jnp.exp(lr * dt)
    ar = mag * jnp.cos(li * dt)
    ai = mag * jnp.sin(li * dt)
    den = lr * lr + li * li
    nr = ar - 1.0
    fr = (nr * lr + ai * li) / den
    fi = (ai * lr - nr * li) / den
    br = b_re.astype(F32)
    bi = b_im.astype(F32)
    bbar_r = fr[..., None] * br - fi[..., None] * bi
    bbar_i = fr[..., None] * bi + fi[..., None] * br
    e = jnp.arange(tc + 1, dtype=F32)
    pmag = jnp.exp(lr[..., None] * dt[..., None] * e)
    pang = li[..., None] * dt[..., None] * e
    pr = pmag * jnp.cos(pang)
    pi = pmag * jnp.sin(pang)
    cr = c_re.astype(F32)
    ci = c_im.astype(F32)
    eye = jnp.eye(gu, dtype=F32)
    ns, ch = gu * P, S5_UNIT

    def unit_c(c):
        return jnp.einsum('ngcp,gh->nhpgc', c.reshape(nu, gu, C, P), eye).reshape(nu, ns, ch)

    def unit_b(b):
        return jnp.einsum('ngpc,gh->ngchp', b.reshape(nu, gu, P, C), eye).reshape(nu, ch, ns)

    ctr, cti = unit_c(cr), unit_c(ci)
    pwr = pr.reshape(2, nu, ns, tc + 1)
    pwi = pi.reshape(2, nu, ns, tc + 1)
    ptr = jnp.transpose(pwr, (0, 3, 1, 2))
    pti = jnp.transpose(pwi, (0, 3, 1, 2))

    ops = {}
    lag = []
    for d in (0, 1):
        btr, bti = unit_b(bbar_r[d]), unit_b(bbar_i[d])
        ba_r = btr[None] * ptr[d][:tc, :, None, :] - bti[None] * pti[d][:tc, :, None, :]
        ba_i = btr[None] * pti[d][:tc, :, None, :] + bti[None] * ptr[d][:tc, :, None, :]
        lag.append(jnp.einsum('tnds,nsc->tndc', ba_r, ctr, precision=hi)
                   - jnp.einsum('tnds,nsc->tndc', ba_i, cti, precision=hi))
        name = "fb"[d]
        ex = (tc - 1 - jnp.arange(tc)) if d == 0 else jnp.arange(tc)
        wr = jnp.transpose(ptr[d][ex], (1, 0, 2))[:, :, None, :]
        wi = jnp.transpose(pti[d][ex], (1, 0, 2))[:, :, None, :]
        bm = jnp.stack([btr[:, None] * wr - bti[:, None] * wi,
                        btr[:, None] * wi + bti[:, None] * wr], axis=3)
        ops["bm" + name] = bm.reshape(nu, tc * ch, 2 * ns).astype(BF16)
        ex = (jnp.arange(tc) + 1) if d == 0 else (tc - jnp.arange(tc))
        er = pwr[d][:, :, ex][..., None]
        ei = pwi[d][:, :, ex][..., None]
        cm = jnp.stack([ctr[:, :, None, :] * er - cti[:, :, None, :] * ei,
                        -(ctr[:, :, None, :] * ei + cti[:, :, None, :] * er)], axis=1)
        ops["cm" + name] = cm.reshape(nu, 2 * ns, tc * ch).astype(BF16)
        ops["a" + name] = jnp.stack([pwr[d][:, :, tc], pwi[d][:, :, tc]], axis=1)
    kf, kb = lag
    dmat = d_skip.astype(F32).reshape(nu, ch)[:, :, None] * jnp.eye(ch, dtype=F32)[None]
    k0 = kf[0] + kb[0] + dmat
    kall = jnp.concatenate([kb[1:][::-1], k0[None], kf[1:]], axis=0)
    tz = jnp.stack([kall[tc - 1 - j:2 * tc - 1 - j] for j in range(tc)], axis=0)
    ops["tz"] = jnp.transpose(tz, (2, 0, 3, 1, 4)).reshape(nu, tc * ch, tc * ch).astype(BF16)
    return ops


def _glu_kernel(y_ref, w_ref, b_ref, g_ref, o_ref):
    y = y_ref[...]
    z = jnp.dot(y.astype(BF16), w_ref[...], preferred_element_type=F32) + b_ref[...]
    o_ref[...] = (y * jax.nn.sigmoid(z) * g_ref[...].astype(F32)).astype(o_ref.dtype)


def _glu(y, w, b, main, widths, bm):
    att, kv, ssm, mem = widths
    T = y.shape[0]
    g_col = (att + 2 * kv + mem + att) // ssm
    return pl.pallas_call(
        _glu_kernel,
        grid=(T // bm,),
        in_specs=[pl.BlockSpec((bm, ssm), lambda m: (m, 0)),
                  pl.BlockSpec((ssm, ssm), lambda m: (0, 0)),
                  pl.BlockSpec((1, ssm), lambda m: (0, 0)),
                  pl.BlockSpec((bm, ssm), lambda m: (m, g_col))],
        out_specs=pl.BlockSpec((bm, ssm), lambda m: (m, 0)),
        out_shape=jax.ShapeDtypeStruct((T, ssm), BF16),
        compiler_params=_params(("parallel",)),
    )(y, w, b, main)


def _out_kernel(att_ref, ssm_ref, mem_ref, wa_ref, ws_ref, wm_ref, x_ref, g_ref, b_ref,
                o_ref, h_ref, *, alpha):
    n = pl.program_id(1)
    nn = pl.num_programs(1)
    h = (jnp.dot(att_ref[...], wa_ref[...], preferred_element_type=F32)
         + jnp.dot(ssm_ref[...], ws_ref[...], preferred_element_type=F32)
         + jnp.dot(mem_ref[...], wm_ref[...], preferred_element_type=F32))
    h_ref[n] = alpha * x_ref[...] + h

    @pl.when(n == nn - 1)
    def _():
        nt, bm, bn = h_ref.shape
        d = nt * bn
        tot = jnp.zeros((bm, 1), F32)
        for j in range(nt):
            tot = tot + jnp.sum(h_ref[j], axis=-1, keepdims=True)
        mu = tot / d
        sq = jnp.zeros((bm, 1), F32)
        for j in range(nt):
            hc = h_ref[j] - mu
            sq = sq + jnp.sum(hc * hc, axis=-1, keepdims=True)
        r = lax.rsqrt(sq / d + LN_EPS)
        for j in range(nt):
            sl = slice(j * bn, (j + 1) * bn)
            o_ref[:, sl] = (h_ref[j] - mu) * r * g_ref[:, sl] + b_ref[:, sl]


def _out_proj(att_o, ssm_o, mem_o, w_out, x, ln_g, ln_b, alpha, bm, bn):
    T, D = x.shape
    att, ssm, mem = att_o.shape[1], ssm_o.shape[1], mem_o.shape[1]
    return pl.pallas_call(
        functools.partial(_out_kernel, alpha=alpha),
        grid=(T // bm, D // bn),
        in_specs=[pl.BlockSpec((bm, att), lambda m, n: (m, 0)),
                  pl.BlockSpec((bm, ssm), lambda m, n: (m, 0)),
                  pl.BlockSpec((bm, mem), lambda m, n: (m, 0)),
                  pl.BlockSpec((att, bn), lambda m, n: (0, n)),
                  pl.BlockSpec((ssm, bn), lambda m, n: (att // ssm, n)),
                  pl.BlockSpec((mem, bn), lambda m, n: ((att + ssm) // mem, n)),
                  pl.BlockSpec((bm, bn), lambda m, n: (m, n)),
                  pl.BlockSpec((1, D), lambda m, n: (0, 0)),
                  pl.BlockSpec((1, D), lambda m, n: (0, 0))],
        out_specs=pl.BlockSpec((bm, D), lambda m, n: (m, 0)),
        out_shape=jax.ShapeDtypeStruct((T, D), F32),
        scratch_shapes=[pltpu.VMEM((D // bn, bm, bn), F32)],
        compiler_params=_params(("parallel", "arbitrary")),
    )(att_o, ssm_o, mem_o, w_out, w_out, w_out, x, ln_g, ln_b)


def _rope_tables(seq_len, gain, scale):
    rows = seq_len // GRID_W
    row = jnp.broadcast_to(jnp.arange(rows, dtype=F32)[:, None], (rows, GRID_W)).reshape(seq_len)
    col = jnp.broadcast_to(jnp.arange(GRID_W, dtype=F32)[None, :], (rows, GRID_W)).reshape(seq_len)
    inv = ROPE_THETA ** (-jnp.arange(0, ROPE_SECTION, 2, dtype=F32) / ROPE_SECTION)
    ang_r = row[:, None] * inv[None, :]
    ang_c = col[:, None] * inv[None, :]
    ang = jnp.concatenate([ang_r, ang_r, ang_c, ang_c], axis=-1)
    cos, sin = jnp.cos(ang), jnp.sin(ang)
    half = ROPE_SECTION // 2
    first = (jnp.arange(HEAD_DIM) % ROPE_SECTION) < half
    g = gain.astype(F32) * scale
    c_t = cos * g[None, :]
    sa_t = jnp.where(first[None, :], -sin * jnp.roll(g, -half)[None, :], 0.0)
    sb_t = jnp.where(first[None, :], 0.0, sin * jnp.roll(g, half)[None, :])
    return c_t, sa_t, sb_t


def _pick(n, pref):
    b = min(n, pref)
    while n % b:
        b //= 2
    return b


def INPROJ_BLOCKS(seq_len):
    return _pick(seq_len, 512), 512


def OUTPROJ_BLOCKS(n_tokens):
    return _pick(n_tokens, 512), 512


def _layer(x3, mem3, w_in_p, q_tabs_fn, widths, ops, w_glu, b_glu, w_mem_kv, w_out, ln_g, ln_b, alpha):
    B, L, D = x3.shape
    att, kv, ssm, mem = widths
    x = x3.reshape(B * L, D)
    n_mem = mem3.shape[1]
    tables = q_tabs_fn(L)
    main, u = _in_proj(x, w_in_p, tables, L, widths, *INPROJ_BLOCKS(L))
    att_o = _attention(main, B, L, widths, *ATTN_BLOCKS(L))
    memkv = _matmul(mem3.reshape(B * n_mem, D), w_mem_kv, 512)
    mem_o = _mem_attention(main, memkv, B, L, n_mem, widths, _pick(L, 512))
    rb = _pick(L // S5_CHUNK, S5_ROW_BLOCK)
    y_f = _s5_pass(u, None, ops, B, L, False, rb)
    y = _s5_pass(u, y_f, ops, B, L, True, rb)
    ssm_o = _glu(y, w_glu, b_glu, main, widths, _pick(B * L, 512))
    out = _out_proj(att_o, ssm_o, mem_o, w_out, x, ln_g, ln_b, alpha, *OUTPROJ_BLOCKS(B * L))
    return out.reshape(B, L, D)


def kernel(x_prompt, x_sample, mem_prompt, mem_sample, w_in, q_norm_g, k_norm_g, ssm_lam_re, ssm_lam_im, ssm_log_step, ssm_b_re, ssm_b_im, ssm_c_re, ssm_c_im, ssm_d, w_glu, b_glu, w_mem_kv, w_out, ln_g, ln_b):
    depth, d_model, in_width = w_in.shape
    att = d_model // 2
    kv = N_KV_HEADS * HEAD_DIM
    ssm = d_model // 4
    mem = d_model - att - ssm
    widths = (att, kv, ssm, mem)
    alpha = (2 * depth) ** 0.25
    o = [0, att, att + kv, att + 2 * kv, 2 * att + 2 * kv, 2 * att + 2 * kv + ssm,
         2 * att + 2 * kv + 2 * ssm, 2 * att + 2 * kv + 2 * ssm + mem, in_width]
    seg = lambda i: (o[i], o[i + 1])
    order = [seg(0), seg(1), seg(2), seg(6), seg(3), seg(5), seg(7), seg(4)]

    y_p, y_s = x_prompt, x_sample
    for l in range(depth):
        w_in_p = jnp.concatenate([w_in[l][:, a:b] for a, b in order], axis=1).astype(BF16)
        att_scale = HEAD_DIM ** -0.5 * LOG2E

        def tabs(L, l=l):
            return (_rope_tables(L, q_norm_g[l], att_scale) + _rope_tables(L, k_norm_g[l], 1.0))

        ops = _s5_operators(ssm_lam_re[l], ssm_lam_im[l], ssm_log_step[l], ssm_b_re[l], ssm_b_im[l],
                            ssm_c_re[l], ssm_c_im[l], ssm_d[l])
        args = (w_in_p, tabs, widths, ops, w_glu[l].astype(BF16), b_glu[l].astype(F32)[None, :],
                w_mem_kv[l].astype(BF16), w_out[l].astype(BF16),
                ln_g[l].astype(F32)[None, :], ln_b[l].astype(F32)[None, :], alpha)
        y_p = _layer(y_p, mem_prompt, *args)
        y_s = _layer(y_s, mem_sample, *args)
    return (y_p, y_s)
```

```python
import functools
import math

import jax
import jax.numpy as jnp
from jax import lax
from jax.experimental import pallas as pl
from jax.experimental.pallas import tpu as pltpu

F32 = jnp.float32
BF16 = jnp.bfloat16

HEAD_DIM = 128
N_KV_HEADS = 4
GRID_W = 64
ROPE_SECTION = HEAD_DIM // 2
ROPE_THETA = 10000.0
RMS_EPS = 1e-6
LN_EPS = 1e-5
SSM_GROUP = 16
SSM_STATE = 64
N_MEM_HEADS = 4
LOG2E = 1.4426950408889634

V7X_LANES = 128
V7X_VMEM_BYTES = 64 * 1024 * 1024
VMEM_LIMIT = V7X_VMEM_BYTES - 6 * 1024 * 1024

S5_CHUNK = 16
S5_UNIT = V7X_LANES
S5_UNIT_STATES = (S5_UNIT // SSM_GROUP) * SSM_STATE
S5_ROW_BLOCK = 256


def _params(sem):
    return pltpu.CompilerParams(dimension_semantics=sem, vmem_limit_bytes=VMEM_LIMIT)


def _silu(x):
    h = 0.5 * x
    return h + h * jnp.tanh(h)


def _in_proj_kernel(x_ref, w_ref, cq_ref, saq_ref, sbq_ref, ck_ref, sak_ref, sbk_ref,
                    o_ref, u_ref, xb_ref, acc_ref, *, n_q, n_qk, n_plain, n_main, n_tiles):
    n = pl.program_id(1)

    def matmul():
        acc_ref[n % 2] = jnp.dot(xb_ref[...], w_ref[...], preferred_element_type=F32)

    def prev():
        return acc_ref[(n + 1) % 2]

    def norm_rope(cos_ref, sa_ref, sb_ref):
        acc = prev()
        for h in range(acc.shape[1] // HEAD_DIM):
            t = acc[:, h * HEAD_DIM:(h + 1) * HEAD_DIM]
            r = lax.rsqrt(jnp.mean(t * t, axis=-1, keepdims=True) + RMS_EPS)
            y = (t * cos_ref[...]
                 + pltpu.roll(t, HEAD_DIM - ROPE_SECTION // 2, 1) * sa_ref[...]
                 + pltpu.roll(t, ROPE_SECTION // 2, 1) * sb_ref[...])
            o_ref[:, h * HEAD_DIM:(h + 1) * HEAD_DIM] = (y * r).astype(o_ref.dtype)

    def rope_q():
        norm_rope(cq_ref, saq_ref, sbq_ref)

    def rope_k():
        norm_rope(ck_ref, sak_ref, sbk_ref)

    def plain():
        o_ref[...] = prev().astype(o_ref.dtype)

    def silu():
        o_ref[...] = _silu(prev()).astype(o_ref.dtype)

    def ssm_input():
        u_ref[...] = prev()

    @pl.when(n == 0)
    def _():
        xb_ref[...] = x_ref[...].astype(BF16)
        matmul()

    bounds = [(0, n_q, rope_q), (n_q, n_qk, rope_k), (n_qk, n_plain, plain),
              (n_plain, n_main, silu), (n_main, n_tiles, ssm_input)]
    for lo, hi, epilogue in bounds:
        last = min(hi, n_tiles - 1)

        @pl.when((n > lo) & (n <= last))
        def _(epilogue=epilogue):
            epilogue()
            matmul()

    @pl.when(n == n_tiles)
    def _():
        bounds[-1][2]()


def _in_proj(x, w, tables, seq_len, widths, bm, bn):
    T, D = x.shape
    att, kv, ssm, mem = widths
    n_q = att // bn
    n_qk = n_q + kv // bn
    n_plain = n_qk + (kv + mem) // bn
    n_main = n_plain + (att + ssm + mem) // bn
    n_u = ssm // bn
    n_tiles = n_main + n_u
    w_main = n_main * bn
    pos_blocks = seq_len // bm
    tab_spec = pl.BlockSpec((bm, HEAD_DIM), lambda m, n: (m % pos_blocks, 0))
    kern = functools.partial(_in_proj_kernel, n_q=n_q, n_qk=n_qk, n_plain=n_plain, n_main=n_main,
                             n_tiles=n_tiles)
    return pl.pallas_call(
        kern,
        grid=(T // bm, n_tiles + 1),
        in_specs=[pl.BlockSpec((bm, D), lambda m, n: (m, 0)),
                  pl.BlockSpec((D, bn), lambda m, n: (0, jnp.minimum(n, n_tiles - 1)))] + [tab_spec] * 6,
        out_specs=[pl.BlockSpec((bm, bn), lambda m, n: (m, jnp.clip(n - 1, 0, n_main - 1))),
                   pl.BlockSpec((bm, bn), lambda m, n: (m, jnp.clip(n - 1 - n_main, 0, n_u - 1)))],
        out_shape=[jax.ShapeDtypeStruct((T, w_main), BF16),
                   jax.ShapeDtypeStruct((T, ssm), F32)],
        scratch_shapes=[pltpu.VMEM((bm, D), BF16), pltpu.VMEM((2, bm, bn), F32)],
        compiler_params=_params(("parallel", "arbitrary")),
        name="in_proj",
    )(x, w, *tables)


def _attn_kernel(q_ref, k_ref, v_ref, g_ref, o_ref, kt_ref, vp_ref, acc_ref, m_ref, *, bk, unroll):
    bq = q_ref.shape[0]
    n_rep = q_ref.shape[1] // HEAD_DIM
    seq = k_ref.shape[0]
    nk = seq // bk

    @pl.when(pl.program_id(2) == 0)
    def _():
        for c in range(seq // bk):
            kt_ref[:, c * bk:(c + 1) * bk] = k_ref[c * bk:(c + 1) * bk, :].T
        vp_ref[:, :HEAD_DIM] = v_ref[...]
        vp_ref[:, HEAD_DIM:] = jnp.ones((seq, HEAD_DIM), vp_ref.dtype)

    acc_ref[...] = jnp.zeros_like(acc_ref)
    m_ref[...] = jnp.full(m_ref.shape, -1e30, F32)

    def body(j, carry):
        off = pl.multiple_of(j * bk, bk)
        kt = kt_ref[:, pl.ds(off, bk)]
        vp = vp_ref[pl.ds(off, bk), :]
        for h in range(n_rep):
            q = q_ref[:, h * HEAD_DIM:(h + 1) * HEAD_DIM]
            s = jnp.dot(q, kt, preferred_element_type=F32)
            m_old = m_ref[h]
            m_new = jnp.maximum(m_old, jnp.max(s, axis=-1, keepdims=True))
            p = jnp.exp2(s - jnp.concatenate([m_new] * (bk // HEAD_DIM), axis=1)).astype(BF16)
            alpha = jnp.exp2(m_old - m_new)
            acc_ref[h] = (jnp.concatenate([alpha, alpha], axis=1) * acc_ref[h]
                          + jnp.dot(p, vp, preferred_element_type=F32))
            m_ref[h] = m_new
        return carry

    lax.fori_loop(0, nk, body, 0, unroll=unroll)
    for h in range(n_rep):
        a = acc_ref[h]
        sl = slice(h * HEAD_DIM, (h + 1) * HEAD_DIM)
        o_ref[:, sl] = (a[:, :HEAD_DIM] / a[:, HEAD_DIM:] * g_ref[:, sl].astype(F32)).astype(o_ref.dtype)


def ATTN_BLOCKS(seq_len):
    bk = _pick(seq_len, 512)
    return _pick(seq_len, 512), bk, math.gcd(seq_len // bk, 4)


def _attention(main, batch, seq_len, widths, bq, bk, unroll):
    att, kv, ssm, mem = widths
    T = main.shape[0]
    n_rep = att // kv
    wq = n_rep * HEAD_DIM
    k_col = att // HEAD_DIM
    v_col = (att + kv) // HEAD_DIM
    g_col = (att + 2 * kv + mem) // wq
    qb = seq_len // bq
    return pl.pallas_call(
        functools.partial(_attn_kernel, bk=bk, unroll=unroll),
        grid=(batch, N_KV_HEADS, qb),
        in_specs=[pl.BlockSpec((bq, wq), lambda b, g, i: (b * qb + i, g)),
                  pl.BlockSpec((seq_len, HEAD_DIM), lambda b, g, i: (b, k_col + g)),
                  pl.BlockSpec((seq_len, HEAD_DIM), lambda b, g, i: (b, v_col + g)),
                  pl.BlockSpec((bq, wq), lambda b, g, i: (b * qb + i, g_col + g))],
        out_specs=pl.BlockSpec((bq, wq), lambda b, g, i: (b * qb + i, g)),
        out_shape=jax.ShapeDtypeStruct((T, att), BF16),
        scratch_shapes=[pltpu.VMEM((HEAD_DIM, seq_len), BF16),
                        pltpu.VMEM((seq_len, 2 * HEAD_DIM), BF16),
                        pltpu.VMEM((n_rep, bq, 2 * HEAD_DIM), F32),
                        pltpu.VMEM((n_rep, bq, HEAD_DIM), F32)],
        compiler_params=_params(("parallel", "parallel", "arbitrary")),
        name="gqa_attention",
    )(main, main, main, main)


def _matmul_kernel(x_ref, w_ref, o_ref):
    o_ref[...] = jnp.dot(x_ref[...].astype(BF16), w_ref[...],
                         preferred_element_type=F32).astype(o_ref.dtype)


def _matmul(x, w, bn):
    M, K = x.shape
    N = w.shape[1]
    return pl.pallas_call(
        _matmul_kernel,
        grid=(N // bn,),
        in_specs=[pl.BlockSpec((M, K), lambda n: (0, 0)),
                  pl.BlockSpec((K, bn), lambda n: (0, n))],
        out_specs=pl.BlockSpec((M, bn), lambda n: (0, n)),
        out_shape=jax.ShapeDtypeStruct((M, N), BF16),
        compiler_params=_params(("arbitrary",)),
        name="mem_kv_proj",
    )(x, w)


def _mem_attn_kernel(q_ref, k_ref, v_ref, g_ref, o_ref, *, scale):
    hd = q_ref.shape[1] // N_MEM_HEADS
    for h in range(N_MEM_HEADS):
        sl = slice(h * hd, (h + 1) * hd)
        q = (q_ref[:, sl].astype(F32) * scale).astype(BF16)
        s = lax.dot_general(q, k_ref[:, sl], (((1,), (1,)), ((), ())), preferred_element_type=F32)
        p = jnp.exp2(s - jnp.max(s, axis=-1, keepdims=True))
        l = jnp.sum(p, axis=-1, keepdims=True)
        o = jnp.dot(p.astype(BF16), v_ref[:, sl], preferred_element_type=F32)
        o_ref[:, sl] = (o / l * g_ref[:, sl].astype(F32)).astype(o_ref.dtype)


def _mem_attention(main, memkv, batch, seq_len, n_mem, widths, bl):
    att, kv, ssm, mem = widths
    T = main.shape[0]
    q_col = (att + 2 * kv) // mem
    g_col = (att + 2 * kv + mem + att + ssm) // mem
    lb = seq_len // bl
    scale = (mem // N_MEM_HEADS) ** -0.5 * LOG2E
    return pl.pallas_call(
        functools.partial(_mem_attn_kernel, scale=scale),
        grid=(batch, lb),
        in_specs=[pl.BlockSpec((bl, mem), lambda b, i: (b * lb + i, q_col)),
                  pl.BlockSpec((n_mem, mem), lambda b, i: (b, 0)),
                  pl.BlockSpec((n_mem, mem), lambda b, i: (b, 1)),
                  pl.BlockSpec((bl, mem), lambda b, i: (b * lb + i, g_col))],
        out_specs=pl.BlockSpec((bl, mem), lambda b, i: (b * lb + i, 0)),
        out_shape=jax.ShapeDtypeStruct((T, mem), BF16),
        compiler_params=_params(("parallel", "arbitrary")),
        name="mem_attention",
    )(main, memkv, memkv, main)


def _gelu_tanh(y):
    return 0.5 * y * (1.0 + jnp.tanh(math.sqrt(2.0 / math.pi) * (y + 0.044715 * (y * y * y))))


def _s5_kernel(*refs, reverse, tc):
    if reverse:
        u_ref, yin_ref, bm_ref, cm_ref, a_ref, y_ref, uc_ref, s_ref, st_ref = refs
    else:
        u_ref, tz_ref, bm_ref, cm_ref, a_ref, y_ref, uc_ref, s_ref, st_ref = refs
    rb = uc_ref.shape[0]
    ns = S5_UNIT_STATES

    @pl.when(pl.program_id(2) == 0)
    def _():
        st_ref[...] = jnp.zeros_like(st_ref)

    for j in range(tc):
        uc_ref[:, j * S5_UNIT:(j + 1) * S5_UNIT] = u_ref[pl.ds(j, rb, stride=tc), :].astype(BF16)
    uc = uc_ref[...]
    s_ref[...] = jnp.dot(uc, bm_ref[...], preferred_element_type=F32)
    ar = a_ref[0:1, :]
    ai = a_ref[1:2, :]

    def body(i, carry):
        xr, xi = carry
        c = (rb - 1 - i) if reverse else i
        sr = s_ref[pl.ds(c, 1), 0:ns]
        si = s_ref[pl.ds(c, 1), ns:2 * ns]
        s_ref[pl.ds(c, 1), 0:ns] = xr
        s_ref[pl.ds(c, 1), ns:2 * ns] = xi
        return ar * xr - ai * xi + sr, ar * xi + ai * xr + si

    xr, xi = lax.fori_loop(0, rb, body, (st_ref[0:1, :], st_ref[1:2, :]), unroll=8)
    st_ref[0:1, :] = xr
    st_ref[1:2, :] = xi

    y = jnp.dot(s_ref[...].astype(BF16), cm_ref[...], preferred_element_type=F32)
    if not reverse:
        y = y + jnp.dot(uc, tz_ref[...], preferred_element_type=F32)
    for t in range(tc):
        rows = pl.ds(t, rb, stride=tc)
        yt = y[:, t * S5_UNIT:(t + 1) * S5_UNIT]
        if reverse:
            y_ref[rows, :] = _gelu_tanh(yin_ref[rows, :] + yt)
        else:
            y_ref[rows, :] = yt


def _s5_pass(u, y_in, ops, batch, seq_len, reverse, rb):
    T, ssm = u.shape
    tc = S5_CHUNK
    n_units = ssm // S5_UNIT
    tok = rb * tc
    nb = seq_len // tok
    ns = S5_UNIT_STATES

    def row(b, r):
        return b * nb + ((nb - 1 - r) if reverse else r)

    tok_spec = pl.BlockSpec((tok, S5_UNIT), lambda n, b, r: (row(b, r), n))
    wspec = lambda shape: pl.BlockSpec((None,) + shape, lambda n, b, r: (n, 0, 0))
    if reverse:
        tz, bm, cm, a = None, ops["bmb"], ops["cmb"], ops["ab"]
        ins = [u, y_in, bm, cm, a]
        in_specs = [tok_spec, tok_spec]
    else:
        tz, bm, cm, a = ops["tz"], ops["bmf"], ops["cmf"], ops["af"]
        ins = [u, tz, bm, cm, a]
        in_specs = [tok_spec, wspec((tc * S5_UNIT, tc * S5_UNIT))]
    in_specs += [wspec((tc * S5_UNIT, 2 * ns)), wspec((2 * ns, tc * S5_UNIT)), wspec((2, ns))]
    return pl.pallas_call(
        functools.partial(_s5_kernel, reverse=reverse, tc=tc),
        grid=(n_units, batch, nb),
        in_specs=in_specs,
        out_specs=tok_spec,
        out_shape=jax.ShapeDtypeStruct((T, ssm), F32),
        scratch_shapes=[pltpu.VMEM((rb, tc * S5_UNIT), BF16),
                        pltpu.VMEM((rb, 2 * ns), F32),
                        pltpu.VMEM((2, ns), F32)],
        compiler_params=_params(("parallel", "arbitrary", "arbitrary")),
        name="s5_backward" if reverse else "s5_forward",
    )(*ins)


def _s5_operators(lam_re, lam_im, log_step, b_re, b_im, c_re, c_im, d_skip):
    hi = lax.Precision.HIGHEST
    tc = S5_CHUNK
    G = lam_re.shape[1]
    P, C = SSM_STATE, SSM_GROUP
    gu = S5_UNIT // C
    nu = G // gu
    lr = lam_re.astype(F32)
    li = lam_im.astype(F32)
    dt = jnp.exp(log_step.astype(F32))[..., None]
    mag = jnp.exp(lr * dt)
    ar = mag * jnp.cos(li * dt)
    ai = mag * jnp.sin(li * dt)
    den = lr * lr + li * li
    nr = ar - 1.0
    fr = (nr * lr + ai * li) / den
    fi = (ai * lr - nr * li) / den
    br = b_re.astype(F32)
    bi = b_im.astype(F32)
    bbar_r = fr[..., None] * br - fi[..., None] * bi
    bbar_i = fr[..., None] * bi + fi[..., None] * br
    e = jnp.arange(tc + 1, dtype=F32)
    pmag = jnp.exp(lr[..., None] * dt[..., None] * e)
    pang = li[..., None] * dt[..., None] * e
    pr = pmag * jnp.cos(pang)
    pi = pmag * jnp.sin(pang)
    cr = c_re.astype(F32)
    ci = c_im.astype(F32)
    eye = jnp.eye(gu, dtype=F32)
    ns, ch = gu * P, S5_UNIT

    def unit_c(c):
        return jnp.einsum('ngcp,gh->nhpgc', c.reshape(nu, gu, C, P), eye).reshape(nu, ns, ch)

    def unit_b(b):
        return jnp.einsum('ngpc,gh->ngchp', b.reshape(nu, gu, P, C), eye).reshape(nu, ch, ns)

    ctr, cti = unit_c(cr), unit_c(ci)
    pwr = pr.reshape(2, nu, ns, tc + 1)
    pwi = pi.reshape(2, nu, ns, tc + 1)
    ptr = jnp.transpose(pwr, (0, 3, 1, 2))
    pti = jnp.transpose(pwi, (0, 3, 1, 2))

    ops = {}
    lag = []
    for d in (0, 1):
        btr, bti = unit_b(bbar_r[d]), unit_b(bbar_i[d])
        ba_r = btr[None] * ptr[d][:tc, :, None, :] - bti[None] * pti[d][:tc, :, None, :]
        ba_i = btr[None] * pti[d][:tc, :, None, :] + bti[None] * ptr[d][:tc, :, None, :]
        lag.append(jnp.einsum('tnds,nsc->tndc', ba_r, ctr, precision=hi)
                   - jnp.einsum('tnds,nsc->tndc', ba_i, cti, precision=hi))
        name = "fb"[d]
        ex = (tc - 1 - jnp.arange(tc)) if d == 0 else jnp.arange(tc)
        wr = jnp.transpose(ptr[d][ex], (1, 0, 2))[:, :, None, :]
        wi = jnp.transpose(pti[d][ex], (1, 0, 2))[:, :, None, :]
        bm = jnp.concatenate([(btr[:, None] * wr - bti[:, None] * wi).astype(BF16),
                              (btr[:, None] * wi + bti[:, None] * wr).astype(BF16)], axis=-1)
        ops["bm" + name] = bm.reshape(nu, tc * ch, 2 * ns)
        ex = [t + 1 if d == 0 else tc - t for t in range(tc)]
        cm_re = jnp.concatenate([(ctr * pwr[d][:, :, e:e + 1] - cti * pwi[d][:, :, e:e + 1]).astype(BF16)
                                 for e in ex], axis=-1)
        cm_im = jnp.concatenate([(-(ctr * pwi[d][:, :, e:e + 1] + cti * pwr[d][:, :, e:e + 1])).astype(BF16)
                                 for e in ex], axis=-1)
        ops["cm" + name] = jnp.concatenate([cm_re, cm_im], axis=1)
        ops["a" + name] = jnp.stack([pwr[d][:, :, tc], pwi[d][:, :, tc]], axis=1)
    kf, kb = lag
    dmat = d_skip.astype(F32).reshape(nu, ch)[:, :, None] * jnp.eye(ch, dtype=F32)[None]
    k0 = kf[0] + kb[0] + dmat
    lags = [kb[tc - 1 - i] for i in range(tc - 1)] + [k0] + [kf[i] for i in range(1, tc)]
    kcat = jnp.concatenate([k.astype(BF16) for k in lags], axis=-1)
    ops["tz"] = jnp.concatenate([kcat[:, :, (tc - 1 - j) * ch:(2 * tc - 1 - j) * ch] for j in range(tc)],
                                axis=1)
    return ops


def _glu_kernel(y_ref, w_ref, b_ref, g_ref, o_ref):
    y = y_ref[...]
    z = jnp.dot(y.astype(BF16), w_ref[...], preferred_element_type=F32) + b_ref[...]
    o_ref[...] = (y * jax.nn.sigmoid(z) * g_ref[...].astype(F32)).astype(o_ref.dtype)


def _glu(y, w, b, main, widths, bm):
    att, kv, ssm, mem = widths
    T = y.shape[0]
    g_col = (att + 2 * kv + mem + att) // ssm
    return pl.pallas_call(
        _glu_kernel,
        grid=(T // bm,),
        in_specs=[pl.BlockSpec((bm, ssm), lambda m: (m, 0)),
                  pl.BlockSpec((ssm, ssm), lambda m: (0, 0)),
                  pl.BlockSpec((1, ssm), lambda m: (0, 0)),
                  pl.BlockSpec((bm, ssm), lambda m: (m, g_col))],
        out_specs=pl.BlockSpec((bm, ssm), lambda m: (m, 0)),
        out_shape=jax.ShapeDtypeStruct((T, ssm), BF16),
        compiler_params=_params(("parallel",)),
        name="glu",
    )(y, w, b, main)


def _out_kernel(att_ref, ssm_ref, mem_ref, wa_ref, ws_ref, wm_ref, x_ref, g_ref, b_ref,
                o_ref, h_ref, *, alpha):
    n = pl.program_id(1)
    nn = pl.num_programs(1)
    h = (jnp.dot(att_ref[...], wa_ref[...], preferred_element_type=F32)
         + jnp.dot(ssm_ref[...], ws_ref[...], preferred_element_type=F32)
         + jnp.dot(mem_ref[...], wm_ref[...], preferred_element_type=F32))
    h_ref[n] = alpha * x_ref[...] + h

    @pl.when(n == nn - 1)
    def _():
        nt, bm, bn = h_ref.shape
        d = nt * bn
        tot = jnp.zeros((bm, 1), F32)
        for j in range(nt):
            tot = tot + jnp.sum(h_ref[j], axis=-1, keepdims=True)
        mu = tot / d
        sq = jnp.zeros((bm, 1), F32)
        for j in range(nt):
            hc = h_ref[j] - mu
            sq = sq + jnp.sum(hc * hc, axis=-1, keepdims=True)
        r = lax.rsqrt(sq / d + LN_EPS)
        for j in range(nt):
            sl = slice(j * bn, (j + 1) * bn)
            o_ref[:, sl] = (h_ref[j] - mu) * r * g_ref[:, sl] + b_ref[:, sl]


def _out_proj(att_o, ssm_o, mem_o, w_out, x, ln_g, ln_b, alpha, bm, bn):
    T, D = x.shape
    att, ssm, mem = att_o.shape[1], ssm_o.shape[1], mem_o.shape[1]
    return pl.pallas_call(
        functools.partial(_out_kernel, alpha=alpha),
        grid=(T // bm, D // bn),
        in_specs=[pl.BlockSpec((bm, att), lambda m, n: (m, 0)),
                  pl.BlockSpec((bm, ssm), lambda m, n: (m, 0)),
                  pl.BlockSpec((bm, mem), lambda m, n: (m, 0)),
                  pl.BlockSpec((att, bn), lambda m, n: (0, n)),
                  pl.BlockSpec((ssm, bn), lambda m, n: (att // ssm, n)),
                  pl.BlockSpec((mem, bn), lambda m, n: ((att + ssm) // mem, n)),
                  pl.BlockSpec((bm, bn), lambda m, n: (m, n)),
                  pl.BlockSpec((1, D), lambda m, n: (0, 0)),
                  pl.BlockSpec((1, D), lambda m, n: (0, 0))],
        out_specs=pl.BlockSpec((bm, D), lambda m, n: (m, 0)),
        out_shape=jax.ShapeDtypeStruct((T, D), F32),
        scratch_shapes=[pltpu.VMEM((D // bn, bm, bn), F32)],
        compiler_params=_params(("parallel", "arbitrary")),
        name="out_proj",
    )(att_o, ssm_o, mem_o, w_out,w_out, w_out, x, ln_g, ln_b)


def _rope_tables(seq_len, gain, scale):
    rows = seq_len // GRID_W
    row = jnp.broadcast_to(jnp.arange(rows, dtype=F32)[:, None], (rows, GRID_W)).reshape(seq_len)
    col = jnp.broadcast_to(jnp.arange(GRID_W, dtype=F32)[None, :], (rows, GRID_W)).reshape(seq_len)
    inv = ROPE_THETA ** (-jnp.arange(0, ROPE_SECTION, 2, dtype=F32) / ROPE_SECTION)
    ang_r = row[:, None] * inv[None, :]
    ang_c = col[:, None] * inv[None, :]
    ang = jnp.concatenate([ang_r, ang_r, ang_c, ang_c], axis=-1)
    cos, sin = jnp.cos(ang), jnp.sin(ang)
    half = ROPE_SECTION // 2
    first = (jnp.arange(HEAD_DIM) % ROPE_SECTION) < half
    g = gain.astype(F32) * scale
    c_t = cos * g[None, :]
    sa_t = jnp.where(first[None, :], -sin * jnp.roll(g, -half)[None, :], 0.0)
    sb_t = jnp.where(first[None, :], 0.0, sin * jnp.roll(g, half)[None, :])
    return c_t, sa_t, sb_t


def _pick(n, pref):
    b = min(n, pref)
    while n % b:
        b //= 2
    return b


def INPROJ_BLOCKS(seq_len):
    return _pick(seq_len, 512), 512


def OUTPROJ_BLOCKS(n_tokens):
    return _pick(n_tokens, 512), 512


def _layer(x3, mem3, w_in_p, q_tabs_fn, widths, ops, w_glu, b_glu, w_mem_kv, w_out, ln_g, ln_b, alpha):
    B, L, D = x3.shape
    att, kv, ssm, mem = widths
    x = x3.reshape(B * L, D)
    n_mem = mem3.shape[1]
    tables = q_tabs_fn(L)
    main, u = _in_proj(x, w_in_p, tables, L, widths, *INPROJ_BLOCKS(L))
    att_o = _attention(main, B, L, widths, *ATTN_BLOCKS(L))
    memkv = _matmul(mem3.reshape(B * n_mem, D), w_mem_kv, 512)
    mem_o = _mem_attention(main, memkv, B, L, n_mem, widths, _pick(L, 512))
    rb = _pick(L // S5_CHUNK, S5_ROW_BLOCK)
    y_f = _s5_pass(u, None, ops, B, L, False, rb)
    y = _s5_pass(u, y_f, ops, B, L, True, rb)
    ssm_o = _glu(y, w_glu, b_glu, main, widths, _pick(B * L, 512))
    out = _out_proj(att_o, ssm_o, mem_o, w_out, x, ln_g, ln_b, alpha, *OUTPROJ_BLOCKS(B * L))
    return out.reshape(B, L, D)


def kernel(x_prompt, x_sample, mem_prompt, mem_sample, w_in, q_norm_g, k_norm_g, ssm_lam_re, ssm_lam_im, ssm_log_step, ssm_b_re, ssm_b_im, ssm_c_re, ssm_c_im, ssm_d, w_glu, b_glu, w_mem_kv, w_out, ln_g, ln_b):
    depth, d_model, in_width = w_in.shape
    att = d_model // 2
    kv = N_KV_HEADS * HEAD_DIM
    ssm = d_model // 4
    mem = d_model - att - ssm
    widths = (att, kv, ssm, mem)
    alpha = (2 * depth) ** 0.25
    o = [0, att, att + kv, att + 2 * kv, 2 * att + 2 * kv, 2 * att + 2 * kv + ssm,
         2 * att + 2 * kv + 2 * ssm, 2 * att + 2 * kv + 2 * ssm + mem, in_width]
    seg = lambda i: (o[i], o[i + 1])
    order = [seg(0), seg(1), seg(2), seg(6), seg(3), seg(5), seg(7), seg(4)]

    y_p, y_s = x_prompt, x_sample
    for l in range(depth):
        w_in_p = jnp.concatenate([w_in[l][:, a:b] for a, b in order], axis=1).astype(BF16)
        att_scale = HEAD_DIM ** -0.5 * LOG2E

        def tabs(L, l=l):
            return (_rope_tables(L, q_norm_g[l], att_scale) + _rope_tables(L, k_norm_g[l], 1.0))

        ops = _s5_operators(ssm_lam_re[l], ssm_lam_im[l], ssm_log_step[l], ssm_b_re[l], ssm_b_im[l],
                            ssm_c_re[l], ssm_c_im[l], ssm_d[l])
        args = (w_in_p, tabs, widths, ops, w_glu[l].astype(BF16), b_glu[l].astype(F32)[None, :],
                w_mem_kv[l].astype(BF16), w_out[l].astype(BF16),
                ln_g[l].astype(F32)[None, :], ln_b[l].astype(F32)[None, :], alpha)
        y_p = _layer(y_p, mem_prompt, *args)
        y_s = _layer(y_s, mem_sample, *args)
    return (y_p, y_s)
```

```python
import functools
import math

import jax
import jax.numpy as jnp
from jax import lax
from jax.experimental import pallas as pl
from jax.experimental.pallas import tpu as pltpu

F32 = jnp.float32
BF16 = jnp.bfloat16

HEAD_DIM = 128
N_KV_HEADS = 4
GRID_W = 64
ROPE_SECTION = HEAD_DIM // 2
ROPE_THETA = 10000.0
RMS_EPS = 1e-6
LN_EPS = 1e-5
SSM_GROUP = 16
SSM_STATE = 64
N_MEM_HEADS = 4
LOG2E = 1.4426950408889634

V7X_LANES = 128
V7X_VMEM_BYTES = 64 * 1024 * 1024
VMEM_LIMIT = V7X_VMEM_BYTES - 6 * 1024 * 1024

S5_CHUNK = 16
S5_UNIT = V7X_LANES
S5_UNIT_STATES = (S5_UNIT // SSM_GROUP) * SSM_STATE
S5_ROW_BLOCK = 256


def _params(sem):
    return pltpu.CompilerParams(dimension_semantics=sem, vmem_limit_bytes=VMEM_LIMIT)


def _silu(x):
    h = 0.5 * x
    return h + h * jnp.tanh(h)


def _in_proj_kernel(x_ref, w_ref, cq_ref, saq_ref, sbq_ref, ck_ref, sak_ref, sbk_ref,
                    o_ref, u_ref, xb_ref, acc_ref, *, n_q, n_qk, n_plain, n_main, n_tiles):
    n = pl.program_id(1)

    def matmul():
        acc_ref[n % 2] = jnp.dot(xb_ref[...], w_ref[...], preferred_element_type=F32)

    def prev():
        return acc_ref[(n + 1) % 2]

    def norm_rope(cos_ref, sa_ref, sb_ref):
        acc = prev()
        for h in range(acc.shape[1] // HEAD_DIM):
            t = acc[:, h * HEAD_DIM:(h + 1) * HEAD_DIM]
            r = lax.rsqrt(jnp.mean(t * t, axis=-1, keepdims=True) + RMS_EPS)
            y = (t * cos_ref[...]
                 + pltpu.roll(t, HEAD_DIM - ROPE_SECTION // 2, 1) * sa_ref[...]
                 + pltpu.roll(t, ROPE_SECTION // 2, 1) * sb_ref[...])
            o_ref[:, h * HEAD_DIM:(h + 1) * HEAD_DIM] = (y * r).astype(o_ref.dtype)

    def rope_q():
        norm_rope(cq_ref, saq_ref, sbq_ref)

    def rope_k():
        norm_rope(ck_ref, sak_ref, sbk_ref)

    def plain():
        o_ref[...] = prev().astype(o_ref.dtype)

    def silu():
        o_ref[...] = _silu(prev()).astype(o_ref.dtype)

    def ssm_input():
        u_ref[...] = prev()

    @pl.when(n == 0)
    def _():
        xb_ref[...] = x_ref[...].astype(BF16)
        matmul()

    bounds = [(0, n_q, rope_q), (n_q, n_qk, rope_k), (n_qk, n_plain, plain),
              (n_plain, n_main, silu), (n_main, n_tiles, ssm_input)]
    for lo, hi, epilogue in bounds:
        last = min(hi, n_tiles - 1)

        @pl.when((n > lo) & (n <= last))
        def _(epilogue=epilogue):
            epilogue()
            matmul()

    @pl.when(n == n_tiles)
    def _():
        bounds[-1][2]()


def _in_proj(x, w, tables, seq_len, widths, bm, bn):
    T, D = x.shape
    att, kv, ssm, mem = widths
    n_q = att // bn
    n_qk = n_q + kv // bn
    n_plain = n_qk + (kv + mem) // bn
    n_main = n_plain + (att + ssm + mem) // bn
    n_u = ssm // bn
    n_tiles = n_main + n_u
    w_main = n_main * bn
    pos_blocks = seq_len // bm
    tab_spec = pl.BlockSpec((bm, HEAD_DIM), lambda m, n: (m % pos_blocks, 0))
    kern = functools.partial(_in_proj_kernel, n_q=n_q, n_qk=n_qk, n_plain=n_plain, n_main=n_main,
                             n_tiles=n_tiles)
    return pl.pallas_call(
        kern,
        grid=(T // bm, n_tiles + 1),
        in_specs=[pl.BlockSpec((bm, D), lambda m, n: (m, 0)),
                  pl.BlockSpec((None, D, bn), lambda m, n: (jnp.minimum(n, n_tiles - 1), 0, 0))] + [tab_spec] * 6,
        out_specs=[pl.BlockSpec((bm, bn), lambda m, n: (m, jnp.clip(n - 1, 0, n_main - 1))),
                   pl.BlockSpec((bm, bn), lambda m, n: (m, jnp.clip(n - 1 - n_main, 0, n_u - 1)))],
        out_shape=[jax.ShapeDtypeStruct((T, w_main), BF16),
                   jax.ShapeDtypeStruct((T, ssm), F32)],
        scratch_shapes=[pltpu.VMEM((bm, D), BF16), pltpu.VMEM((2, bm, bn), F32)],
        compiler_params=_params(("parallel", "arbitrary")),
        name="in_proj",
    )(x, w, *tables)


def _attn_kernel(q_ref, k_ref, v_ref, g_ref, o_ref, kt_ref, vp_ref, acc_ref, m_ref, *, bk, unroll):
    bq = q_ref.shape[0]
    n_rep = q_ref.shape[1] // HEAD_DIM
    seq = k_ref.shape[0]
    nk = seq // bk

    @pl.when(pl.program_id(2) == 0)
    def _():
        for c in range(seq // bk):
            kt_ref[:, c * bk:(c + 1) * bk] = k_ref[c * bk:(c + 1) * bk, :].T
        vp_ref[:, :HEAD_DIM] = v_ref[...]
        vp_ref[:, HEAD_DIM:] = jnp.ones((seq, HEAD_DIM), vp_ref.dtype)

    acc_ref[...] = jnp.zeros_like(acc_ref)
    m_ref[...] = jnp.full(m_ref.shape, -1e30, F32)

    def body(j, carry):
        off = pl.multiple_of(j * bk, bk)
        kt = kt_ref[:, pl.ds(off, bk)]
        vp = vp_ref[pl.ds(off, bk), :]
        for h in range(n_rep):
            q = q_ref[:, h * HEAD_DIM:(h + 1) * HEAD_DIM]
            s = jnp.dot(q, kt, preferred_element_type=F32)
            m_old = m_ref[h]
            m_new = jnp.maximum(m_old, jnp.max(s, axis=-1, keepdims=True))
            p = jnp.exp2(s - jnp.concatenate([m_new] * (bk // HEAD_DIM), axis=1)).astype(BF16)
            alpha = jnp.exp2(m_old - m_new)
            acc_ref[h] = (jnp.concatenate([alpha, alpha], axis=1) * acc_ref[h]
                          + jnp.dot(p, vp, preferred_element_type=F32))
            m_ref[h] = m_new
        return carry

    lax.fori_loop(0, nk, body, 0, unroll=unroll)
    for h in range(n_rep):
        a = acc_ref[h]
        sl = slice(h * HEAD_DIM, (h + 1) * HEAD_DIM)
        o_ref[:, sl] = (a[:, :HEAD_DIM] / a[:, HEAD_DIM:] * g_ref[:, sl].astype(F32)).astype(o_ref.dtype)


def ATTN_BLOCKS(seq_len):
    bk = _pick(seq_len, 512)
    return _pick(seq_len, 512), bk, math.gcd(seq_len // bk, 4)


def _attention(main, batch, seq_len, widths, bq, bk, unroll):
    att, kv, ssm, mem = widths
    T = main.shape[0]
    n_rep = att // kv
    wq = n_rep * HEAD_DIM
    k_col = att // HEAD_DIM
    v_col = (att + kv) // HEAD_DIM
    g_col = (att + 2 * kv + mem) // wq
    qb = seq_len // bq
    return pl.pallas_call(
        functools.partial(_attn_kernel, bk=bk, unroll=unroll),
        grid=(batch, N_KV_HEADS, qb),
        in_specs=[pl.BlockSpec((bq, wq), lambda b, g, i: (b * qb + i, g)),
                  pl.BlockSpec((seq_len, HEAD_DIM), lambda b, g, i: (b, k_col + g)),
                  pl.BlockSpec((seq_len, HEAD_DIM), lambda b, g, i: (b, v_col + g)),
                  pl.BlockSpec((bq, wq), lambda b, g, i: (b * qb + i, g_col + g))],
        out_specs=pl.BlockSpec((bq, wq), lambda b, g, i: (b * qb + i, g)),
        out_shape=jax.ShapeDtypeStruct((T, att), BF16),
        scratch_shapes=[pltpu.VMEM((HEAD_DIM, seq_len), BF16),
                        pltpu.VMEM((seq_len, 2 * HEAD_DIM), BF16),
                        pltpu.VMEM((n_rep, bq, 2 * HEAD_DIM), F32),
                        pltpu.VMEM((n_rep, bq, HEAD_DIM), F32)],
        compiler_params=_params(("parallel", "parallel", "arbitrary")),
        name="gqa_attention",
    )(main, main, main, main)


def _matmul_kernel(x_ref, w_ref, o_ref):
    o_ref[...] = jnp.dot(x_ref[...].astype(BF16), w_ref[...],
                         preferred_element_type=F32).astype(o_ref.dtype)


def _matmul(x, w, bn):
    M, K = x.shape
    N = w.shape[1]
    return pl.pallas_call(
        _matmul_kernel,
        grid=(N // bn,),
        in_specs=[pl.BlockSpec((M, K), lambda n: (0, 0)),
                  pl.BlockSpec((K, bn), lambda n: (0, n))],
        out_specs=pl.BlockSpec((M, bn), lambda n: (0, n)),
        out_shape=jax.ShapeDtypeStruct((M, N), BF16),
        compiler_params=_params(("arbitrary",)),
        name="mem_kv_proj",
    )(x, w)


def _mem_attn_kernel(q_ref, k_ref, v_ref, g_ref, o_ref, *, scale):
    hd = q_ref.shape[1] // N_MEM_HEADS
    for h in range(N_MEM_HEADS):
        sl = slice(h * hd, (h + 1) * hd)
        q = (q_ref[:, sl].astype(F32) * scale).astype(BF16)
        s = lax.dot_general(q, k_ref[:, sl], (((1,), (1,)), ((), ())), preferred_element_type=F32)
        p = jnp.exp2(s - jnp.max(s, axis=-1, keepdims=True))
        l = jnp.sum(p, axis=-1, keepdims=True)
        o = jnp.dot(p.astype(BF16), v_ref[:, sl], preferred_element_type=F32)
        o_ref[:, sl] = (o / l * g_ref[:, sl].astype(F32)).astype(o_ref.dtype)


def _mem_attention(main, memkv, batch, seq_len, n_mem, widths, bl):
    att, kv, ssm, mem = widths
    T = main.shape[0]
    q_col = (att + 2 * kv) // mem
    g_col = (att + 2 * kv + mem + att + ssm) // mem
    lb = seq_len // bl
    scale = (mem // N_MEM_HEADS) ** -0.5 * LOG2E
    return pl.pallas_call(
        functools.partial(_mem_attn_kernel, scale=scale),
        grid=(batch, lb),
        in_specs=[pl.BlockSpec((bl, mem), lambda b, i: (b * lb + i, q_col)),
                  pl.BlockSpec((n_mem, mem), lambda b, i: (b, 0)),
                  pl.BlockSpec((n_mem, mem), lambda b, i: (b, 1)),
                  pl.BlockSpec((bl, mem), lambda b, i: (b * lb + i, g_col))],
        out_specs=pl.BlockSpec((bl, mem), lambda b, i: (b * lb + i, 0)),
        out_shape=jax.ShapeDtypeStruct((T, mem), BF16),
        compiler_params=_params(("parallel", "arbitrary")),
        name="mem_attention",
    )(main, memkv, memkv, main)


def _gelu_tanh(y):
    return 0.5 * y * (1.0 + jnp.tanh(math.sqrt(2.0 / math.pi) * (y + 0.044715 * (y * y * y))))


def _s5_kernel(*refs, reverse, tc):
    if reverse:
        u_ref, yin_ref, bm_ref, cm_ref, a_ref, y_ref, uc_ref, s_ref, st_ref = refs
    else:
        u_ref, tz_ref, bm_ref, cm_ref, a_ref, y_ref, uc_ref, s_ref, st_ref = refs
    rb = uc_ref.shape[0]
    ns = S5_UNIT_STATES

    @pl.when(pl.program_id(2) == 0)
    def _():
        st_ref[...] = jnp.zeros_like(st_ref)

    for j in range(tc):
        uc_ref[:, j * S5_UNIT:(j + 1) * S5_UNIT] = u_ref[pl.ds(j, rb, stride=tc), :].astype(BF16)
    uc = uc_ref[...]
    s_ref[...] = jnp.dot(uc, bm_ref[...], preferred_element_type=F32)
    ar = a_ref[0:1, :]
    ai = a_ref[1:2, :]

    def body(i, carry):
        xr, xi = carry
        c = (rb - 1 - i) if reverse else i
        sr = s_ref[pl.ds(c, 1), 0:ns]
        si = s_ref[pl.ds(c, 1), ns:2 * ns]
        s_ref[pl.ds(c, 1), 0:ns] = xr
        s_ref[pl.ds(c, 1), ns:2 * ns] = xi
        return ar * xr - ai * xi + sr, ar * xi + ai * xr + si

    xr, xi = lax.fori_loop(0, rb, body, (st_ref[0:1, :], st_ref[1:2, :]), unroll=8)
    st_ref[0:1, :] = xr
    st_ref[1:2, :] = xi

    y = jnp.dot(s_ref[...].astype(BF16), cm_ref[...], preferred_element_type=F32)
    if not reverse:
        y = y + jnp.dot(uc, tz_ref[...], preferred_element_type=F32)
    for t in range(tc):
        rows = pl.ds(t, rb, stride=tc)
        yt = y[:, t * S5_UNIT:(t + 1) * S5_UNIT]
        if reverse:
            y_ref[rows, :] = _gelu_tanh(yin_ref[rows, :] + yt)
        else:
            y_ref[rows, :] = yt


def _s5_pass(u, y_in, ops, batch, seq_len, reverse, rb):
    T, ssm = u.shape
    tc = S5_CHUNK
    n_units = ssm // S5_UNIT
    tok = rb * tc
    nb = seq_len // tok
    ns = S5_UNIT_STATES

    def row(b, r):
        return b * nb + ((nb - 1 - r) if reverse else r)

    tok_spec = pl.BlockSpec((tok, S5_UNIT), lambda n, b, r: (row(b, r), n))
    wspec = lambda shape: pl.BlockSpec((None,) + shape, lambda n, b, r: (n, 0, 0))
    if reverse:
        tz, bm, cm, a = None, ops["bmb"], ops["cmb"], ops["ab"]
        ins = [u, y_in, bm, cm, a]
        in_specs = [tok_spec, tok_spec]
    else:
        tz, bm, cm, a = ops["tz"], ops["bmf"], ops["cmf"], ops["af"]
        ins = [u, tz, bm, cm, a]
        in_specs = [tok_spec, wspec((tc * S5_UNIT, tc * S5_UNIT))]
    in_specs += [wspec((tc * S5_UNIT, 2 * ns)), wspec((2 * ns, tc * S5_UNIT)), wspec((2, ns))]
    return pl.pallas_call(
        functools.partial(_s5_kernel, reverse=reverse, tc=tc),
        grid=(n_units, batch, nb),
        in_specs=in_specs,
        out_specs=tok_spec,
        out_shape=jax.ShapeDtypeStruct((T, ssm), F32),
        scratch_shapes=[pltpu.VMEM((rb, tc * S5_UNIT), BF16),
                        pltpu.VMEM((rb, 2 * ns), F32),
                        pltpu.VMEM((2, ns), F32)],
        compiler_params=_params(("parallel", "arbitrary", "arbitrary")),
        name="s5_backward" if reverse else "s5_forward",
    )(*ins)


def _s5_operators(lam_re, lam_im, log_step, b_re, b_im, c_re, c_im, d_skip):
    hi = lax.Precision.HIGHEST
    tc = S5_CHUNK
    G = lam_re.shape[1]
    P, C = SSM_STATE, SSM_GROUP
    gu = S5_UNIT // C
    nu = G // gu
    lr = lam_re.astype(F32)
    li = lam_im.astype(F32)
    dt = jnp.exp(log_step.astype(F32))[..., None]
    mag = jnp.exp(lr * dt)
    ar = mag * jnp.cos(li * dt)
    ai = mag * jnp.sin(li * dt)
    den = lr * lr + li * li
    nr = ar - 1.0
    fr = (nr * lr + ai * li) / den
    fi = (ai * lr - nr * li) / den
    br = b_re.astype(F32)
    bi = b_im.astype(F32)
    bbar_r = fr[..., None] * br - fi[..., None] * bi
    bbar_i = fr[..., None] * bi + fi[..., None] * br
    e = jnp.arange(tc + 1, dtype=F32)
    pmag = jnp.exp(lr[..., None] * dt[..., None] * e)
    pang = li[..., None] * dt[..., None] * e
    pr = pmag * jnp.cos(pang)
    pi = pmag * jnp.sin(pang)
    cr = c_re.astype(F32)
    ci = c_im.astype(F32)
    eye = jnp.eye(gu, dtype=F32)
    ns, ch = gu * P, S5_UNIT

    def unit_c(c):
        return jnp.einsum('ngcp,gh->nhpgc', c.reshape(nu, gu, C, P), eye).reshape(nu, ns, ch)

    def unit_b(b):
        return jnp.einsum('ngpc,gh->ngchp', b.reshape(nu, gu, P, C), eye).reshape(nu, ch, ns)

    ctr, cti = unit_c(cr), unit_c(ci)
    pwr = pr.reshape(2, nu, ns, tc + 1)
    pwi = pi.reshape(2, nu, ns, tc + 1)
    ptr = jnp.transpose(pwr, (0, 3, 1, 2))
    pti = jnp.transpose(pwi, (0, 3, 1, 2))

    ops = {}
    lag = []
    for d in (0, 1):
        btr, bti = unit_b(bbar_r[d]), unit_b(bbar_i[d])
        ba_r = btr[None] * ptr[d][:tc, :, None, :] - bti[None] * pti[d][:tc, :, None, :]
        ba_i = btr[None] * pti[d][:tc, :, None, :] + bti[None] * ptr[d][:tc, :, None, :]
        lag.append(jnp.einsum('tnds,nsc->tndc', ba_r, ctr, precision=hi)
                   - jnp.einsum('tnds,nsc->tndc', ba_i, cti, precision=hi))
        name = "fb"[d]
        ex = (tc - 1 - jnp.arange(tc)) if d == 0 else jnp.arange(tc)
        wr = jnp.transpose(ptr[d][ex], (1, 0, 2))[:, :, None, :]
        wi = jnp.transpose(pti[d][ex], (1, 0, 2))[:, :, None, :]
        bm = jnp.concatenate([(btr[:, None] * wr - bti[:, None] * wi).astype(BF16),
                              (btr[:, None] * wi + bti[:, None] * wr).astype(BF16)], axis=-1)
        ops["bm" + name] = bm.reshape(nu, tc * ch, 2 * ns)
        ex = [t + 1 if d == 0 else tc - t for t in range(tc)]
        cm_re = jnp.concatenate([(ctr * pwr[d][:, :, e:e + 1] - cti * pwi[d][:, :, e:e + 1]).astype(BF16)
                                 for e in ex], axis=-1)
        cm_im = jnp.concatenate([(-(ctr * pwi[d][:, :, e:e + 1] + cti * pwr[d][:, :, e:e + 1])).astype(BF16)
                                 for e in ex], axis=-1)
        ops["cm" + name] = jnp.concatenate([cm_re, cm_im], axis=1)
        ops["a" + name] = jnp.stack([pwr[d][:, :, tc], pwi[d][:, :, tc]], axis=1)
    kf, kb = lag
    dmat = d_skip.astype(F32).reshape(nu, ch)[:, :, None] * jnp.eye(ch, dtype=F32)[None]
    k0 = kf[0] + kb[0] + dmat
    lags = [kb[tc - 1 - i] for i in range(tc - 1)] + [k0] + [kf[i] for i in range(1, tc)]
    kcat = jnp.concatenate([k.astype(BF16) for k in lags], axis=-1)
    ops["tz"] = jnp.concatenate([kcat[:, :, (tc - 1 - j) * ch:(2 * tc - 1 - j) * ch] for j in range(tc)],
                                axis=1)
    return ops


def _glu_kernel(y_ref, w_ref, b_ref, g_ref, o_ref):
    y = y_ref[...]
    z = jnp.dot(y.astype(BF16), w_ref[...], preferred_element_type=F32) + b_ref[...]
    o_ref[...] = (y * jax.nn.sigmoid(z) * g_ref[...].astype(F32)).astype(o_ref.dtype)


def _glu(y, w, b, main, widths, bm):
    att, kv, ssm, mem = widths
    T = y.shape[0]
    g_col = (att + 2 * kv + mem + att) // ssm
    return pl.pallas_call(
        _glu_kernel,
        grid=(T // bm,),
        in_specs=[pl.BlockSpec((bm, ssm), lambda m: (m, 0)),
                  pl.BlockSpec((ssm, ssm), lambda m: (0, 0)),
                  pl.BlockSpec((1, ssm), lambda m: (0, 0)),
                  pl.BlockSpec((bm, ssm), lambda m: (m, g_col))],
        out_specs=pl.BlockSpec((bm, ssm), lambda m: (m, 0)),
        out_shape=jax.ShapeDtypeStruct((T, ssm), BF16),
        compiler_params=_params(("parallel",)),
        name="glu",
    )(y, w, b, main)


def _out_kernel(att_ref, ssm_ref, mem_ref, wa_ref, ws_ref, wm_ref, x_ref, g_ref, b_ref,
                o_ref, h_ref, *, alpha):
    n = pl.program_id(1)
    nn = pl.num_programs(1)
    h = (jnp.dot(att_ref[...], wa_ref[...], preferred_element_type=F32)
         + jnp.dot(ssm_ref[...], ws_ref[...], preferred_element_type=F32)
         + jnp.dot(mem_ref[...], wm_ref[...], preferred_element_type=F32))
    h_ref[n] = alpha * x_ref[...] + h

    @pl.when(n == nn - 1)
    def _():
        nt, bm, bn = h_ref.shape
        d = nt * bn
        tot = jnp.zeros((bm, 1), F32)
        for j in range(nt):
            tot = tot + jnp.sum(h_ref[j], axis=-1, keepdims=True)
        mu = tot / d
        sq = jnp.zeros((bm, 1), F32)
        for j in range(nt):
            hc = h_ref[j] - mu
            sq = sq + jnp.sum(hc * hc, axis=-1, keepdims=True)
        r = lax.rsqrt(sq / d + LN_EPS)
        for j in range(nt):
            sl = slice(j * bn, (j + 1) * bn)
            o_ref[:, sl] = (h_ref[j] - mu) * r * g_ref[:, sl] + b_ref[:, sl]


def _out_proj(att_o, ssm_o, mem_o, w_out, x, ln_g, ln_b, alpha, bm, bn):
    T, D = x.shape
    att, ssm, mem = att_o.shape[1], ssm_o.shape[1], mem_o.shape[1]
    return pl.pallas_call(
        functools.partial(_out_kernel, alpha=alpha),
        grid=(T // bm, D // bn),
        in_specs=[pl.BlockSpec((bm, att), lambda m, n: (m, 0)),
                  pl.BlockSpec((bm, ssm), lambda m, n: (m, 0)),
                  pl.BlockSpec((bm, mem), lambda m, n: (m, 0)),
                  pl.BlockSpec((None, att, bn), lambda m, n: (n, 0, 0)),
                  pl.BlockSpec((None, ssm, bn), lambda m, n: (n, att // ssm, 0)),
                  pl.BlockSpec((None, mem, bn), lambda m, n: (n, (att + ssm) // mem, 0)),
                  pl.BlockSpec((bm, bn), lambda m, n: (m, n)),
                  pl.BlockSpec((1, D), lambda m, n: (0, 0)),
                  pl.BlockSpec((1, D), lambda m, n: (0, 0))],
        out_specs=pl.BlockSpec((bm, D), lambda m, n: (m, 0)),
        out_shape=jax.ShapeDtypeStruct((T, D), F32),
        scratch_shapes=[pltpu.VMEM((D // bn, bm, bn), F32)],
        compiler_params=_params(("parallel", "arbitrary")),
        name="out_proj",
    )(att_o, ssm_o, mem_o, w_out, w_out, w_out, x, ln_g, ln_b)


def _rope_tables(seq_len, gain, scale):
    rows = seq_len // GRID_W
    row = jnp.broadcast_to(jnp.arange(rows, dtype=F32)[:, None], (rows, GRID_W)).reshape(seq_len)
    col = jnp.broadcast_to(jnp.arange(GRID_W, dtype=F32)[None, :], (rows, GRID_W)).reshape(seq_len)
    inv = ROPE_THETA ** (-jnp.arange(0, ROPE_SECTION, 2, dtype=F32) / ROPE_SECTION)
    ang_r = row[:, None] * inv[None, :]
    ang_c = col[:, None] * inv[None, :]
    ang = jnp.concatenate([ang_r, ang_r, ang_c, ang_c], axis=-1)
    cos, sin = jnp.cos(ang), jnp.sin(ang)
    half = ROPE_SECTION // 2
    first = (jnp.arange(HEAD_DIM) % ROPE_SECTION) < half
    g = gain.astype(F32) * scale
    c_t = cos * g[None, :]
    sa_t = jnp.where(first[None, :], -sin * jnp.roll(g, -half)[None, :], 0.0)
    sb_t = jnp.where(first[None, :], 0.0, sin * jnp.roll(g, half)[None, :])
    return c_t, sa_t, sb_t


def _column_tiles(w, bn):
    k, n = w.shape
    return w.reshape(k, n // bn, bn).transpose(1, 0, 2)


def _pick(n, pref):
    b = min(n, pref)
    while n % b:
        b //= 2
    return b


IN_BN = 512
OUT_BN = 512


def INPROJ_BLOCKS(seq_len):
    return _pick(seq_len, 512), IN_BN


def OUTPROJ_BLOCKS(n_tokens):
    return _pick(n_tokens, 512), OUT_BN


def _layer(x3, mem3, w_in_p, q_tabs_fn, widths, ops, w_glu, b_glu, w_mem_kv, w_out, ln_g, ln_b, alpha):
    B, L, D = x3.shape
    att, kv, ssm, mem = widths
    x = x3.reshape(B * L, D)
    n_mem = mem3.shape[1]
    tables = q_tabs_fn(L)
    main, u = _in_proj(x, w_in_p, tables, L, widths, *INPROJ_BLOCKS(L))
    att_o = _attention(main, B, L, widths, *ATTN_BLOCKS(L))
    memkv = _matmul(mem3.reshape(B * n_mem, D), w_mem_kv, 512)
    mem_o = _mem_attention(main, memkv, B, L, n_mem, widths, _pick(L, 512))
    rb = _pick(L // S5_CHUNK, S5_ROW_BLOCK)
    y_f = _s5_pass(u, None, ops, B, L, False, rb)
    y = _s5_pass(u, y_f, ops, B, L, True, rb)
    ssm_o = _glu(y, w_glu, b_glu, main, widths, _pick(B * L, 512))
    out = _out_proj(att_o, ssm_o, mem_o, w_out, x, ln_g, ln_b, alpha, *OUTPROJ_BLOCKS(B * L))
    return out.reshape(B, L, D)


def kernel(x_prompt, x_sample, mem_prompt, mem_sample, w_in, q_norm_g, k_norm_g, ssm_lam_re, ssm_lam_im, ssm_log_step, ssm_b_re, ssm_b_im, ssm_c_re, ssm_c_im, ssm_d, w_glu, b_glu, w_mem_kv, w_out, ln_g, ln_b):
    depth, d_model, in_width = w_in.shape
    att = d_model // 2
    kv = N_KV_HEADS * HEAD_DIM
    ssm = d_model // 4
    mem = d_model - att - ssm
    widths = (att, kv, ssm, mem)
    alpha = (2 * depth) ** 0.25
    o = [0, att, att + kv, att + 2 * kv, 2 * att + 2 * kv, 2 * att + 2 * kv + ssm,
         2 * att + 2 * kv + 2 * ssm, 2 * att + 2 * kv + 2 * ssm + mem, in_width]
    seg = lambda i: (o[i], o[i + 1])
    order = [seg(0), seg(1), seg(2), seg(6), seg(3), seg(5), seg(7), seg(4)]

    y_p, y_s = x_prompt, x_sample
    for l in range(depth):
        w_in_p = jnp.concatenate([w_in[l][:, a:b] for a, b in order], axis=1).astype(BF16)
        w_in_p = _column_tiles(w_in_p, IN_BN)
        att_scale = HEAD_DIM ** -0.5 * LOG2E

        def tabs(L, l=l):
            return (_rope_tables(L, q_norm_g[l], att_scale) + _rope_tables(L, k_norm_g[l], 1.0))

        ops = _s5_operators(ssm_lam_re[l], ssm_lam_im[l], ssm_log_step[l], ssm_b_re[l], ssm_b_im[l],
                            ssm_c_re[l], ssm_c_im[l], ssm_d[l])
        args = (w_in_p, tabs, widths, ops, w_glu[l].astype(BF16), b_glu[l].astype(F32)[None, :],
                w_mem_kv[l].astype(BF16), _column_tiles(w_out[l].astype(BF16), OUT_BN),
                ln_g[l].astype(F32)[None, :], ln_b[l].astype(F32)[None, :], alpha)
        y_p = _layer(y_p, mem_prompt, *args)
        y_s = _layer(y_s, mem_sample, *args)
    return (y_p, y_s)
```

```python
import functools
import math

import jax
import jax.numpy as jnp
from jax import lax
from jax.experimental import pallas as pl
from jax.experimental.pallas import tpu as pltpu

F32 = jnp.float32
BF16 = jnp.bfloat16

HEAD_DIM = 128
N_KV_HEADS = 4
GRID_W = 64
ROPE_SECTION = HEAD_DIM // 2
ROPE_THETA = 10000.0
RMS_EPS = 1e-6
LN_EPS = 1e-5
SSM_GROUP = 16
SSM_STATE = 64
N_MEM_HEADS = 4
LOG2E = 1.4426950408889634

V7X_LANES = 128
V7X_VMEM_BYTES = 64 * 1024 * 1024
VMEM_LIMIT = V7X_VMEM_BYTES - 6 * 1024 * 1024

S5_CHUNK = 16
S5_UNIT = V7X_LANES
S5_UNIT_STATES = (S5_UNIT // SSM_GROUP) * SSM_STATE
S5_ROW_BLOCK = 256


def _params(sem):
    return pltpu.CompilerParams(dimension_semantics=sem, vmem_limit_bytes=VMEM_LIMIT)


def _silu(x):
    h = 0.5 * x
    return h + h * jnp.tanh(h)


def _in_proj_kernel(x_ref, w_ref, cq_ref, saq_ref, sbq_ref, ck_ref, sak_ref, sbk_ref,
                    o_ref, u_ref, xb_ref, acc_ref, *, n_q, n_qk, n_plain, n_main, n_tiles, kv_heads):
    n = pl.program_id(1)

    def matmul():
        acc_ref[n % 2] = jnp.dot(xb_ref[...], w_ref[...], preferred_element_type=F32)

    def prev():
        return acc_ref[(n + 1) % 2]

    def norm_rope(cos_ref, sa_ref, sb_ref, heads):
        for h in range(heads):
            t = acc_ref[(n + 1) % 2, :, h * HEAD_DIM:(h + 1) * HEAD_DIM]
            r = lax.rsqrt(jnp.mean(t * t, axis=-1, keepdims=True) + RMS_EPS)
            y = (t * cos_ref[...]
                 + pltpu.roll(t, HEAD_DIM - ROPE_SECTION // 2, 1) * sa_ref[...]
                 + pltpu.roll(t, ROPE_SECTION // 2, 1) * sb_ref[...])
            o_ref[:, h * HEAD_DIM:(h + 1) * HEAD_DIM] = (y * r).astype(o_ref.dtype)

    def rope_q():
        norm_rope(cq_ref, saq_ref, sbq_ref, o_ref.shape[1] // HEAD_DIM)

    def rope_k_and_v():
        norm_rope(ck_ref, sak_ref, sbk_ref, kv_heads)
        o_ref[:, kv_heads * HEAD_DIM:] = acc_ref[(n + 1) % 2, :, kv_heads * HEAD_DIM:].astype(o_ref.dtype)

    def plain():
        o_ref[...] = prev().astype(o_ref.dtype)

    def silu():
        o_ref[...] = _silu(prev()).astype(o_ref.dtype)

    def ssm_input():
        u_ref[...] = prev()

    @pl.when(n == 0)
    def _():
        xb_ref[...] = x_ref[...].astype(BF16)
        matmul()

    bounds = [(0, n_q, rope_q), (n_q, n_qk, rope_k_and_v), (n_qk, n_plain, plain),
              (n_plain, n_main, silu), (n_main, n_tiles, ssm_input)]
    for lo, hi, epilogue in bounds:
        last = min(hi, n_tiles - 1)

        @pl.when((n > lo) & (n <= last))
        def _(epilogue=epilogue):
            epilogue()
            matmul()

    @pl.when(n == n_tiles)
    def _():
        bounds[-1][2]()


def _in_proj(x, w, tables, seq_len, widths, bm, bn):
    T, D = x.shape
    att, kv, ssm, mem = widths
    assert bn == 2 * kv and att % bn == 0 and ssm % bn == 0 and mem % bn == 0
    n_q = att // bn
    n_qk = n_q + 1
    n_plain = n_qk + mem // bn
    n_main = n_plain + (att + ssm + mem) // bn
    n_u = ssm // bn
    n_tiles = n_main + n_u
    w_main = n_main * bn
    pos_blocks = seq_len // bm
    tab_spec = pl.BlockSpec((bm, HEAD_DIM), lambda m, n: (m % pos_blocks, 0))
    kern = functools.partial(_in_proj_kernel, n_q=n_q, n_qk=n_qk, n_plain=n_plain, n_main=n_main,
                             n_tiles=n_tiles, kv_heads=kv // HEAD_DIM)
    return pl.pallas_call(
        kern,
        grid=(T // bm, n_tiles + 1),
        in_specs=[pl.BlockSpec((bm, D), lambda m, n: (m, 0)),
                  pl.BlockSpec((D, bn), lambda m, n: (0, jnp.minimum(n, n_tiles - 1)))] + [tab_spec] * 6,
        out_specs=[pl.BlockSpec((bm, bn), lambda m, n: (m, jnp.clip(n - 1, 0, n_main - 1))),
                   pl.BlockSpec((bm, bn), lambda m, n: (m, jnp.clip(n - 1 - n_main, 0, n_u - 1)))],
        out_shape=[jax.ShapeDtypeStruct((T, w_main), BF16),
                   jax.ShapeDtypeStruct((T, ssm), F32)],
        scratch_shapes=[pltpu.VMEM((bm, D), BF16), pltpu.VMEM((2, bm, bn), F32)],
        compiler_params=_params(("parallel", "arbitrary")),
        name="in_proj",
    )(x, w, *tables)


def _attn_kernel(q_ref, k_ref, v_ref, g_ref, o_ref, kt_ref, vp_ref, acc_ref, m_ref, *, bk, unroll):
    bq = q_ref.shape[0]
    n_rep = q_ref.shape[1] // HEAD_DIM
    seq = k_ref.shape[0]
    nk = seq // bk

    @pl.when(pl.program_id(2) == 0)
    def _():
        for c in range(seq // bk):
            kt_ref[:, c * bk:(c + 1) * bk] = k_ref[c * bk:(c + 1) * bk, :].T
        vp_ref[:, :HEAD_DIM] = v_ref[...]
        vp_ref[:, HEAD_DIM:] = jnp.ones((seq, HEAD_DIM), vp_ref.dtype)

    acc_ref[...] = jnp.zeros_like(acc_ref)
    m_ref[...] = jnp.full(m_ref.shape, -1e30, F32)

    def body(j, carry):
        off = pl.multiple_of(j * bk, bk)
        kt = kt_ref[:, pl.ds(off, bk)]
        vp = vp_ref[pl.ds(off, bk), :]
        for h in range(n_rep):
            q = q_ref[:, h * HEAD_DIM:(h + 1) * HEAD_DIM]
            s = jnp.dot(q, kt, preferred_element_type=F32)
            m_old = m_ref[h]
            m_new = jnp.maximum(m_old, jnp.max(s, axis=-1, keepdims=True))
            p = jnp.exp2(s - jnp.concatenate([m_new] * (bk // HEAD_DIM), axis=1)).astype(BF16)
            alpha = jnp.exp2(m_old - m_new)
            acc_ref[h] = (jnp.concatenate([alpha, alpha], axis=1) * acc_ref[h]
                          + jnp.dot(p, vp, preferred_element_type=F32))
            m_ref[h] = m_new
        return carry

    lax.fori_loop(0, nk, body, 0, unroll=unroll)
    for h in range(n_rep):
        a = acc_ref[h]
        sl = slice(h * HEAD_DIM, (h + 1) * HEAD_DIM)
        o_ref[:, sl] = (a[:, :HEAD_DIM] / a[:, HEAD_DIM:] * g_ref[:, sl].astype(F32)).astype(o_ref.dtype)


def ATTN_BLOCKS(seq_len):
    bk = _pick(seq_len, 512)
    return _pick(seq_len, 1024), bk, math.gcd(seq_len // bk, 4)


def _attention(main, batch, seq_len, widths, bq, bk, unroll):
    att, kv, ssm, mem = widths
    T = main.shape[0]
    n_rep = att // kv
    wq = n_rep * HEAD_DIM
    k_col = att // HEAD_DIM
    v_col = (att + kv) // HEAD_DIM
    g_col = (att + 2 * kv + mem) // wq
    qb = seq_len // bq
    return pl.pallas_call(
        functools.partial(_attn_kernel, bk=bk, unroll=unroll),
        grid=(batch, N_KV_HEADS, qb),
        in_specs=[pl.BlockSpec((bq, wq), lambda b, g, i: (b * qb + i, g)),
                  pl.BlockSpec((seq_len, HEAD_DIM), lambda b, g, i: (b, k_col + g)),
                  pl.BlockSpec((seq_len, HEAD_DIM), lambda b, g, i: (b, v_col + g)),
                  pl.BlockSpec((bq, wq), lambda b, g, i: (b * qb + i, g_col + g))],
        out_specs=pl.BlockSpec((bq, wq), lambda b, g, i: (b * qb + i, g)),
        out_shape=jax.ShapeDtypeStruct((T, att), BF16),
        scratch_shapes=[pltpu.VMEM((HEAD_DIM, seq_len), BF16),
                        pltpu.VMEM((seq_len, 2 * HEAD_DIM), BF16),
                        pltpu.VMEM((n_rep, bq, 2 * HEAD_DIM), F32),
                        pltpu.VMEM((n_rep, bq, HEAD_DIM), F32)],
        compiler_params=_params(("parallel", "parallel", "arbitrary")),
        name="gqa_attention",
    )(main, main, main, main)


def _matmul_kernel(x_ref, w_ref, o_ref):
    o_ref[...] = jnp.dot(x_ref[...].astype(BF16), w_ref[...],
                         preferred_element_type=F32).astype(o_ref.dtype)


def _matmul(x, w, bn):
    M, K = x.shape
    N = w.shape[1]
    return pl.pallas_call(
        _matmul_kernel,
        grid=(N // bn,),
        in_specs=[pl.BlockSpec((M, K), lambda n: (0, 0)),
                  pl.BlockSpec((K, bn), lambda n: (0, n))],
        out_specs=pl.BlockSpec((M, bn), lambda n: (0, n)),
        out_shape=jax.ShapeDtypeStruct((M, N), BF16),
        compiler_params=_params(("arbitrary",)),
        name="mem_kv_proj",
    )(x, w)


def _mem_attn_kernel(q_ref, k_ref, v_ref, g_ref, o_ref, *, scale):
    hd = q_ref.shape[1] // N_MEM_HEADS
    for h in range(N_MEM_HEADS):
        sl = slice(h * hd, (h + 1) * hd)
        q = (q_ref[:, sl].astype(F32) * scale).astype(BF16)
        s = lax.dot_general(q, k_ref[:, sl], (((1,), (1,)), ((), ())), preferred_element_type=F32)
        p = jnp.exp2(s - jnp.max(s, axis=-1, keepdims=True))
        l = jnp.sum(p, axis=-1, keepdims=True)
        o = jnp.dot(p.astype(BF16), v_ref[:, sl], preferred_element_type=F32)
        o_ref[:, sl] = (o / l * g_ref[:, sl].astype(F32)).astype(o_ref.dtype)


def _mem_attention(main, memkv, batch, seq_len, n_mem, widths, bl):
    att, kv, ssm, mem = widths
    T = main.shape[0]
    q_col = (att + 2 * kv) // mem
    g_col = (att + 2 * kv + mem + att + ssm) // mem
    lb = seq_len // bl
    scale = (mem // N_MEM_HEADS) ** -0.5 * LOG2E
    return pl.pallas_call(
        functools.partial(_mem_attn_kernel, scale=scale),
        grid=(batch, lb),
        in_specs=[pl.BlockSpec((bl, mem), lambda b, i: (b * lb + i, q_col)),
                  pl.BlockSpec((n_mem, mem), lambda b, i: (b, 0)),
                  pl.BlockSpec((n_mem, mem), lambda b, i: (b, 1)),
                  pl.BlockSpec((bl, mem), lambda b, i: (b * lb + i, g_col))],
        out_specs=pl.BlockSpec((bl, mem), lambda b, i: (b * lb + i, 0)),
        out_shape=jax.ShapeDtypeStruct((T, mem), BF16),
        compiler_params=_params(("parallel", "arbitrary")),
        name="mem_attention",
    )(main, memkv, memkv, main)


def _gelu_tanh(y):
    return 0.5 * y * (1.0 + jnp.tanh(math.sqrt(2.0 / math.pi) * (y + 0.044715 * (y * y * y))))


def _s5_kernel(*refs, reverse, tc):
    if reverse:
        u_ref, yin_ref, bm_ref, cm_ref, a_ref, y_ref, uc_ref, s_ref, st_ref = refs
    else:
        u_ref, tz_ref, bm_ref, cm_ref, a_ref, y_ref, uc_ref, s_ref, st_ref = refs
    rb = uc_ref.shape[0]
    ns = S5_UNIT_STATES

    @pl.when(pl.program_id(2) == 0)
    def _():
        st_ref[...] = jnp.zeros_like(st_ref)

    for j in range(tc):
        uc_ref[:, j * S5_UNIT:(j + 1) * S5_UNIT] = u_ref[pl.ds(j, rb, stride=tc), :].astype(BF16)
    uc = uc_ref[...]
    s_ref[...] = jnp.dot(uc, bm_ref[...], preferred_element_type=F32)
    ar = a_ref[0:1, :]
    ai = a_ref[1:2, :]

    def body(i, carry):
        xr, xi = carry
        c = (rb - 1 - i) if reverse else i
        sr = s_ref[pl.ds(c, 1), 0:ns]
        si = s_ref[pl.ds(c, 1), ns:2 * ns]
        s_ref[pl.ds(c, 1), 0:ns] = xr
        s_ref[pl.ds(c, 1), ns:2 * ns] = xi
        return ar * xr - ai * xi + sr, ar * xi + ai * xr + si

    xr, xi = lax.fori_loop(0, rb, body, (st_ref[0:1, :], st_ref[1:2, :]), unroll=8)
    st_ref[0:1, :] = xr
    st_ref[1:2, :] = xi

    y = jnp.dot(s_ref[...].astype(BF16), cm_ref[...], preferred_element_type=F32)
    if not reverse:
        y = y + jnp.dot(uc, tz_ref[...], preferred_element_type=F32)
    for t in range(tc):
        rows = pl.ds(t, rb, stride=tc)
        yt = y[:, t * S5_UNIT:(t + 1) * S5_UNIT]
        if reverse:
            y_ref[rows, :] = _gelu_tanh(yin_ref[rows, :] + yt)
        else:
            y_ref[rows, :] = yt


def _s5_pass(u, y_in, ops, batch, seq_len, reverse, rb):
    T, ssm = u.shape
    tc = S5_CHUNK
    n_units = ssm // S5_UNIT
    tok = rb * tc
    nb = seq_len // tok
    ns = S5_UNIT_STATES

    def row(b, r):
        return b * nb + ((nb - 1 - r) if reverse else r)

    tok_spec = pl.BlockSpec((tok, S5_UNIT), lambda n, b, r: (row(b, r), n))
    wspec = lambda shape: pl.BlockSpec((None,) + shape, lambda n, b, r: (n, 0, 0))
    if reverse:
        tz, bm, cm, a = None, ops["bmb"], ops["cmb"], ops["ab"]
        ins = [u, y_in, bm, cm, a]
        in_specs = [tok_spec, tok_spec]
    else:
        tz, bm, cm, a = ops["tz"], ops["bmf"], ops["cmf"], ops["af"]
        ins = [u, tz, bm, cm, a]
        in_specs = [tok_spec, wspec((tc * S5_UNIT, tc * S5_UNIT))]
    in_specs += [wspec((tc * S5_UNIT, 2 * ns)), wspec((2 * ns, tc * S5_UNIT)), wspec((2, ns))]
    return pl.pallas_call(
        functools.partial(_s5_kernel, reverse=reverse, tc=tc),
        grid=(n_units, batch, nb),
        in_specs=in_specs,
        out_specs=tok_spec,
        out_shape=jax.ShapeDtypeStruct((T, ssm), F32),
        scratch_shapes=[pltpu.VMEM((rb, tc * S5_UNIT), BF16),
                        pltpu.VMEM((rb, 2 * ns), F32),
                        pltpu.VMEM((2, ns), F32)],
        compiler_params=_params(("parallel", "arbitrary", "arbitrary")),
        name="s5_backward" if reverse else "s5_forward",
    )(*ins)


def _s5_operators(lam_re, lam_im, log_step, b_re, b_im, c_re, c_im, d_skip):
    hi = lax.Precision.HIGHEST
    tc = S5_CHUNK
    G = lam_re.shape[1]
    P, C = SSM_STATE, SSM_GROUP
    gu = S5_UNIT // C
    nu = G // gu
    lr = lam_re.astype(F32)
    li = lam_im.astype(F32)
    dt = jnp.exp(log_step.astype(F32))[..., None]
    mag = jnp.exp(lr * dt)
    ar = mag * jnp.cos(li * dt)
    ai = mag * jnp.sin(li * dt)
    den = lr * lr + li * li
    nr = ar - 1.0
    fr = (nr * lr + ai * li) / den
    fi = (ai * lr - nr * li) / den
    br = b_re.astype(F32)
    bi = b_im.astype(F32)
    bbar_r = fr[..., None] * br - fi[..., None] * bi
    bbar_i = fr[..., None] * bi + fi[..., None] * br
    e = jnp.arange(tc + 1, dtype=F32)
    pmag = jnp.exp(lr[..., None] * dt[..., None] * e)
    pang = li[..., None] * dt[..., None] * e
    pr = pmag * jnp.cos(pang)
    pi = pmag * jnp.sin(pang)
    cr = c_re.astype(F32)
    ci = c_im.astype(F32)
    eye = jnp.eye(gu, dtype=F32)
    ns, ch = gu * P, S5_UNIT

    def unit_c(c):
        return jnp.einsum('ngcp,gh->nhpgc', c.reshape(nu, gu, C, P), eye).reshape(nu, ns, ch)

    def unit_b(b):
        return jnp.einsum('ngpc,gh->ngchp', b.reshape(nu, gu, P, C), eye).reshape(nu, ch, ns)

    ctr, cti = unit_c(cr), unit_c(ci)
    pwr = pr.reshape(2, nu, ns, tc + 1)
    pwi = pi.reshape(2, nu, ns, tc + 1)
    ptr = jnp.transpose(pwr, (0, 3, 1, 2))
    pti = jnp.transpose(pwi, (0, 3, 1, 2))

    ops = {}
    lag = []
    for d in (0, 1):
        btr, bti = unit_b(bbar_r[d]), unit_b(bbar_i[d])
        ba_r = btr[None] * ptr[d][:tc, :, None, :] - bti[None] * pti[d][:tc, :, None, :]
        ba_i = btr[None] * pti[d][:tc, :, None, :] + bti[None] * ptr[d][:tc, :, None, :]
        lag.append(jnp.einsum('tnds,nsc->tndc', ba_r, ctr, precision=hi)
                   - jnp.einsum('tnds,nsc->tndc', ba_i, cti, precision=hi))
        name = "fb"[d]
        ex = (tc - 1 - jnp.arange(tc)) if d == 0 else jnp.arange(tc)
        wr = jnp.transpose(ptr[d][ex], (1, 0, 2))[:, :, None, :]
        wi = jnp.transpose(pti[d][ex], (1, 0, 2))[:, :, None, :]
        bm = jnp.concatenate([(btr[:, None] * wr - bti[:, None] * wi).astype(BF16),
                              (btr[:, None] * wi + bti[:, None] * wr).astype(BF16)], axis=-1)
        ops["bm" + name] = bm.reshape(nu, tc * ch, 2 * ns)
        ex = [t + 1 if d == 0 else tc - t for t in range(tc)]
        cm_re = jnp.concatenate([(ctr * pwr[d][:, :, e:e + 1] - cti * pwi[d][:, :, e:e + 1]).astype(BF16)
                                 for e in ex], axis=-1)
        cm_im = jnp.concatenate([(-(ctr * pwi[d][:, :, e:e + 1] + cti * pwr[d][:, :, e:e + 1])).astype(BF16)
                                 for e in ex], axis=-1)
        ops["cm" + name] = jnp.concatenate([cm_re, cm_im], axis=1)
        ops["a" + name] = jnp.stack([pwr[d][:, :, tc], pwi[d][:, :, tc]], axis=1)
    kf, kb = lag
    dmat = d_skip.astype(F32).reshape(nu, ch)[:, :, None] * jnp.eye(ch, dtype=F32)[None]
    k0 = kf[0] + kb[0] + dmat
    lags = [kb[tc - 1 - i] for i in range(tc - 1)] + [k0] + [kf[i] for i in range(1, tc)]
    kcat = jnp.concatenate([k.astype(BF16) for k in lags], axis=-1)
    ops["tz"] = jnp.concatenate([kcat[:, :, (tc - 1 - j) * ch:(2 * tc - 1 - j) * ch] for j in range(tc)],
                                axis=1)
    return ops


def _glu_kernel(y_ref, w_ref, b_ref, g_ref, o_ref):
    y = y_ref[...]
    z = jnp.dot(y.astype(BF16), w_ref[...], preferred_element_type=F32) + b_ref[...]
    o_ref[...] = (y * jax.nn.sigmoid(z) * g_ref[...].astype(F32)).astype(o_ref.dtype)


def _glu(y, w, b, main, widths, bm):
    att, kv, ssm, mem = widths
    T = y.shape[0]
    g_col = (att + 2 * kv + mem + att) // ssm
    return pl.pallas_call(
        _glu_kernel,
        grid=(T // bm,),
        in_specs=[pl.BlockSpec((bm, ssm), lambda m: (m, 0)),
                  pl.BlockSpec((ssm, ssm), lambda m: (0, 0)),
                  pl.BlockSpec((1, ssm), lambda m: (0, 0)),
                  pl.BlockSpec((bm, ssm), lambda m: (m, g_col))],
        out_specs=pl.BlockSpec((bm, ssm), lambda m: (m, 0)),
        out_shape=jax.ShapeDtypeStruct((T, ssm), BF16),
        compiler_params=_params(("parallel",)),
        name="glu",
    )(y, w, b, main)


def _out_kernel(att_ref, ssm_ref, mem_ref, wa_ref, ws_ref, wm_ref, x_ref, g_ref, b_ref,
                o_ref, h_ref, *, alpha):
    n = pl.program_id(1)
    nn = pl.num_programs(1)
    h = (jnp.dot(att_ref[...], wa_ref[...], preferred_element_type=F32)
         + jnp.dot(ssm_ref[...], ws_ref[...], preferred_element_type=F32)
         + jnp.dot(mem_ref[...], wm_ref[...], preferred_element_type=F32))
    h_ref[n] = alpha * x_ref[...] + h

    @pl.when(n == nn - 1)
    def _():
        nt, bm, bn = h_ref.shape
        d = nt * bn
        tot = jnp.zeros((bm, 1), F32)
        for j in range(nt):
            tot = tot + jnp.sum(h_ref[j], axis=-1, keepdims=True)
        mu = tot / d
        sq = jnp.zeros((bm, 1), F32)
        for j in range(nt):
            hc = h_ref[j] - mu
            sq = sq + jnp.sum(hc * hc, axis=-1, keepdims=True)
        r = lax.rsqrt(sq / d + LN_EPS)
        for j in range(nt):
            sl = slice(j * bn, (j + 1) * bn)
            o_ref[:, sl] = (h_ref[j] - mu) * r * g_ref[:, sl] + b_ref[:, sl]


def _out_proj(att_o, ssm_o, mem_o, w_out, x, ln_g, ln_b, alpha, bm, bn):
    T, D = x.shape
    att, ssm, mem = att_o.shape[1], ssm_o.shape[1], mem_o.shape[1]
    return pl.pallas_call(
        functools.partial(_out_kernel, alpha=alpha),
        grid=(T // bm, D // bn),
        in_specs=[pl.BlockSpec((bm, att), lambda m, n: (m, 0)),
                  pl.BlockSpec((bm, ssm), lambda m, n: (m, 0)),
                  pl.BlockSpec((bm, mem), lambda m, n: (m, 0)),
                  pl.BlockSpec((att, bn), lambda m, n: (0, n)),
                  pl.BlockSpec((ssm, bn), lambda m, n: (att // ssm, n)),
                  pl.BlockSpec((mem, bn), lambda m, n: ((att + ssm) // mem, n)),
                  pl.BlockSpec((bm, bn), lambda m, n: (m, n)),
                  pl.BlockSpec((1, D), lambda m, n: (0, 0)),
                  pl.BlockSpec((1, D), lambda m, n: (0, 0))],
        out_specs=pl.BlockSpec((bm, D), lambda m, n: (m, 0)),
        out_shape=jax.ShapeDtypeStruct((T, D), F32),
        scratch_shapes=[pltpu.VMEM((D // bn, bm, bn), F32)],
        compiler_params=_params(("parallel", "arbitrary")),
        name="out_proj",
    )(att_o, ssm_o, mem_o, w_out, w_out, w_out, x, ln_g, ln_b)


def _rope_tables(seq_len, gain, scale):
    rows = seq_len // GRID_W
    row = jnp.broadcast_to(jnp.arange(rows, dtype=F32)[:, None], (rows, GRID_W)).reshape(seq_len)
    col = jnp.broadcast_to(jnp.arange(GRID_W, dtype=F32)[None, :], (rows, GRID_W)).reshape(seq_len)
    inv = ROPE_THETA ** (-jnp.arange(0, ROPE_SECTION, 2, dtype=F32) / ROPE_SECTION)
    ang_r = row[:, None] * inv[None, :]
    ang_c = col[:, None] * inv[None, :]
    ang = jnp.concatenate([ang_r, ang_r, ang_c, ang_c], axis=-1)
    cos, sin = jnp.cos(ang), jnp.sin(ang)
    half = ROPE_SECTION // 2
    first = (jnp.arange(HEAD_DIM) % ROPE_SECTION) < half
    g = gain.astype(F32) * scale
    c_t = cos * g[None, :]
    sa_t = jnp.where(first[None, :], -sin * jnp.roll(g, -half)[None, :], 0.0)
    sb_t = jnp.where(first[None, :], 0.0, sin * jnp.roll(g, half)[None, :])
    return c_t, sa_t, sb_t


def _pick(n, pref):
    b = min(n, pref)
    while n % b:
        b //= 2
    return b


IN_BN = 2 * N_KV_HEADS * HEAD_DIM
OUT_BN = 512


def INPROJ_BLOCKS(seq_len):
    return _pick(seq_len, 512), IN_BN


def OUTPROJ_BLOCKS(n_tokens):
    return _pick(n_tokens, 512), OUT_BN


def _layer(x3, mem3, w_in_p, q_tabs_fn, widths, ops, w_glu, b_glu, w_mem_kv, w_out, ln_g, ln_b, alpha):
    B, L, D = x3.shape
    att, kv, ssm, mem = widths
    x = x3.reshape(B * L, D)
    n_mem = mem3.shape[1]
    tables = q_tabs_fn(L)
    main, u = _in_proj(x, w_in_p, tables, L, widths, *INPROJ_BLOCKS(L))
    att_o = _attention(main, B, L, widths, *ATTN_BLOCKS(L))
    memkv = _matmul(mem3.reshape(B * n_mem, D), w_mem_kv, 512)
    mem_o = _mem_attention(main, memkv, B, L, n_mem, widths, _pick(L, 512))
    rb = _pick(L // S5_CHUNK, S5_ROW_BLOCK)
    y_f = _s5_pass(u, None, ops, B, L, False, rb)
    y = _s5_pass(u, y_f, ops, B, L, True, rb)
    ssm_o = _glu(y, w_glu, b_glu, main, widths, _pick(B * L, 512))
    out = _out_proj(att_o, ssm_o, mem_o, w_out, x, ln_g, ln_b, alpha, *OUTPROJ_BLOCKS(B * L))
    return out.reshape(B, L, D)


def kernel(x_prompt, x_sample, mem_prompt, mem_sample, w_in, q_norm_g, k_norm_g, ssm_lam_re, ssm_lam_im, ssm_log_step, ssm_b_re, ssm_b_im, ssm_c_re, ssm_c_im, ssm_d, w_glu, b_glu, w_mem_kv, w_out, ln_g, ln_b):
    depth, d_model, in_width = w_in.shape
    att = d_model // 2
    kv = N_KV_HEADS * HEAD_DIM
    ssm = d_model // 4
    mem = d_model - att - ssm
    widths = (att, kv, ssm, mem)
    alpha = (2 * depth) ** 0.25
    o = [0, att, att + kv, att + 2 * kv, 2 * att + 2 * kv, 2 * att + 2 * kv + ssm,
         2 * att + 2 * kv + 2 * ssm, 2 * att + 2 * kv + 2 * ssm + mem, in_width]
    seg = lambda i: (o[i], o[i + 1])
    order = [seg(0), seg(1), seg(2), seg(6), seg(3), seg(5), seg(7), seg(4)]

    y_p, y_s = x_prompt, x_sample
    for l in range(depth):
        w_in_p = jnp.concatenate([w_in[l][:, a:b] for a, b in order], axis=1).astype(BF16)
        att_scale = HEAD_DIM ** -0.5 * LOG2E

        def tabs(L, l=l):
            return (_rope_tables(L, q_norm_g[l], att_scale) + _rope_tables(L, k_norm_g[l], 1.0))

        ops = _s5_operators(ssm_lam_re[l], ssm_lam_im[l], ssm_log_step[l], ssm_b_re[l], ssm_b_im[l],
                            ssm_c_re[l], ssm_c_im[l], ssm_d[l])
        args = (w_in_p, tabs, widths, ops, w_glu[l].astype(BF16), b_glu[l].astype(F32)[None, :],
                w_mem_kv[l].astype(BF16), w_out[l].astype(BF16),
                ln_g[l].astype(F32)[None, :], ln_b[l].astype(F32)[None, :], alpha)
        y_p = _layer(y_p, mem_prompt, *args)
        y_s = _layer(y_s, mem_sample, *args)
    return (y_p, y_s)
```

```python
import functools
import math

import jax
import jax.numpy as jnp
from jax import lax
from jax.experimental import pallas as pl
from jax.experimental.pallas import tpu as pltpu

F32 = jnp.float32
BF16 = jnp.bfloat16

HEAD_DIM = 128
N_KV_HEADS = 4
GRID_W = 64
ROPE_SECTION = HEAD_DIM // 2
ROPE_THETA = 10000.0
RMS_EPS = 1e-6
LN_EPS = 1e-5
SSM_GROUP = 16
SSM_STATE = 64
N_MEM_HEADS = 4
LOG2E = 1.4426950408889634

V7X_LANES = 128
V7X_VMEM_BYTES = 64 * 1024 * 1024
VMEM_LIMIT = V7X_VMEM_BYTES - 6 * 1024 * 1024

S5_CHUNK = 16
S5_UNIT = V7X_LANES
S5_UNIT_STATES = (S5_UNIT // SSM_GROUP) * SSM_STATE
S5_ROW_BLOCK = 256


def _params(sem):
    return pltpu.CompilerParams(dimension_semantics=sem, vmem_limit_bytes=VMEM_LIMIT)


def _silu(x):
    h = 0.5 * x
    return h + h * jnp.tanh(h)


def _in_proj_kernel(x_ref, w_ref, cq_ref, saq_ref, sbq_ref, ck_ref, sak_ref, sbk_ref,
                    o_ref, u_ref, xb_ref, acc_ref, *, n_q, n_qk, n_plain, n_main, n_tiles, kv_heads):
    g = pl.program_id(0)
    last = pl.num_programs(0) - 1
    n = g % n_tiles
    cur, prv = g % 2, (g + 1) % 2

    def matmul():
        acc_ref[cur] = jnp.dot(xb_ref[...], w_ref[...], preferred_element_type=F32)

    def cast_rows():
        xb_ref[...] = x_ref[...].astype(BF16)

    def norm_rope(cos_ref, sa_ref, sb_ref, heads):
        for h in range(heads):
            t = acc_ref[prv, :, h * HEAD_DIM:(h + 1) * HEAD_DIM]
            r = lax.rsqrt(jnp.mean(t * t, axis=-1, keepdims=True) + RMS_EPS)
            y = (t * cos_ref[...]
                 + pltpu.roll(t, HEAD_DIM - ROPE_SECTION // 2, 1) * sa_ref[...]
                 + pltpu.roll(t, ROPE_SECTION // 2, 1) * sb_ref[...])
            o_ref[:, h * HEAD_DIM:(h + 1) * HEAD_DIM] = (y * r).astype(o_ref.dtype)

    def rope_q():
        norm_rope(cq_ref, saq_ref, sbq_ref, o_ref.shape[1] // HEAD_DIM)

    def rope_k_and_v():
        norm_rope(ck_ref, sak_ref, sbk_ref, kv_heads)
        o_ref[:, kv_heads * HEAD_DIM:] = acc_ref[prv, :, kv_heads * HEAD_DIM:].astype(o_ref.dtype)

    def plain():
        o_ref[...] = acc_ref[prv].astype(o_ref.dtype)

    def silu():
        o_ref[...] = _silu(acc_ref[prv]).astype(o_ref.dtype)

    def ssm_input():
        u_ref[...] = acc_ref[prv]

    @pl.when(g == 0)
    def _():
        cast_rows()
        matmul()

    @pl.when((n == 0) & (g > 0) & (g < last))
    def _():
        ssm_input()
        cast_rows()
        matmul()

    @pl.when(g == last)
    def _():
        ssm_input()

    bounds = [(0, n_q, rope_q), (n_q, n_qk, rope_k_and_v), (n_qk, n_plain, plain),
              (n_plain, n_main, silu), (n_main, n_tiles, ssm_input)]
    for lo, hi, epilogue in bounds:
        hi = min(hi, n_tiles - 1)

        @pl.when((n > lo) & (n <= hi) & (g < last))
        def _(epilogue=epilogue):
            epilogue()
            matmul()


def _in_proj(x, w, tables, seq_len, widths, bm, bn):
    T, D = x.shape
    att, kv, ssm, mem = widths
    assert bn == 2 * kv and att % bn == 0 and ssm % bn == 0 and mem % bn == 0
    n_q = att // bn
    n_qk = n_q + 1
    n_plain = n_qk + mem // bn
    n_main = n_plain + (att + ssm + mem) // bn
    n_u = ssm // bn
    n_tiles = n_main + n_u
    w_main = n_main * bn
    row_blocks = T // bm
    pos_blocks = seq_len // bm
    steps = row_blocks * n_tiles + 1

    def row(g):
        return jnp.minimum(g // n_tiles, row_blocks - 1)

    def done(g):
        gp = jnp.maximum(g - 1, 0)
        return gp // n_tiles, gp % n_tiles

    tab_spec = pl.BlockSpec((bm, HEAD_DIM), lambda g: (row(g) % pos_blocks, 0))
    kern = functools.partial(_in_proj_kernel, n_q=n_q, n_qk=n_qk, n_plain=n_plain, n_main=n_main,
                             n_tiles=n_tiles, kv_heads=kv // HEAD_DIM)
    return pl.pallas_call(
        kern,
        grid=(steps,),
        in_specs=[pl.BlockSpec((bm, D), lambda g: (row(g), 0)),
                  pl.BlockSpec((D, bn), lambda g: (0, jnp.where(g == steps - 1, n_tiles - 1, g % n_tiles)))]
                 + [tab_spec] * 6,
        out_specs=[pl.BlockSpec((bm, bn), lambda g: (done(g)[0], jnp.minimum(done(g)[1], n_main - 1))),
                   pl.BlockSpec((bm, bn), lambda g: (done(g)[0], jnp.clip(done(g)[1] - n_main, 0, n_u - 1)))],
        out_shape=[jax.ShapeDtypeStruct((T, w_main), BF16),
                   jax.ShapeDtypeStruct((T, ssm), F32)],
        scratch_shapes=[pltpu.VMEM((bm, D), BF16), pltpu.VMEM((2, bm, bn), F32)],
        compiler_params=_params(("arbitrary",)),
        name="in_proj",
    )(x, w, *tables)


def _attn_kernel(q_ref, k_ref, v_ref, g_ref, o_ref, kt_ref, vp_ref, acc_ref, m_ref, *, bk, unroll):
    bq = q_ref.shape[0]
    n_rep = q_ref.shape[1] // HEAD_DIM
    seq = k_ref.shape[0]
    nk = seq // bk

    @pl.when(pl.program_id(2) == 0)
    def _():
        for c in range(seq // bk):
            kt_ref[:, c * bk:(c + 1) * bk] = k_ref[c * bk:(c + 1) * bk, :].T
        vp_ref[:, :HEAD_DIM] = v_ref[...]
        vp_ref[:, HEAD_DIM:] = jnp.ones((seq, HEAD_DIM), vp_ref.dtype)

    acc_ref[...] = jnp.zeros_like(acc_ref)
    m_ref[...] = jnp.full(m_ref.shape, -1e30, F32)

    def body(j, carry):
        off = pl.multiple_of(j * bk, bk)
        kt = kt_ref[:, pl.ds(off, bk)]
        vp = vp_ref[pl.ds(off, bk), :]
        for h in range(n_rep):
            q = q_ref[:, h * HEAD_DIM:(h + 1) * HEAD_DIM]
            s = jnp.dot(q, kt, preferred_element_type=F32)
            m_old = m_ref[h]
            m_new = jnp.maximum(m_old, jnp.max(s, axis=-1, keepdims=True))
            p = jnp.exp2(s - jnp.concatenate([m_new] * (bk // HEAD_DIM), axis=1)).astype(BF16)
            alpha = jnp.exp2(m_old - m_new)
            acc_ref[h] = (jnp.concatenate([alpha, alpha], axis=1) * acc_ref[h]
                          + jnp.dot(p, vp, preferred_element_type=F32))
            m_ref[h] = m_new
        return carry

    lax.fori_loop(0, nk, body, 0, unroll=unroll)
    for h in range(n_rep):
        a = acc_ref[h]
        sl = slice(h * HEAD_DIM, (h + 1) * HEAD_DIM)
        o_ref[:, sl] = (a[:, :HEAD_DIM] / a[:, HEAD_DIM:] * g_ref[:, sl].astype(F32)).astype(o_ref.dtype)


def ATTN_BLOCKS(seq_len):
    bk = _pick(seq_len, 512)
    return _pick(seq_len, 1024), bk, math.gcd(seq_len // bk, 4)


def _attention(main, batch, seq_len, widths, bq, bk, unroll):
    att, kv, ssm, mem = widths
    T = main.shape[0]
    n_rep = att // kv
    wq = n_rep * HEAD_DIM
    k_col = att // HEAD_DIM
    v_col = (att + kv) // HEAD_DIM
    g_col = (att + 2 * kv + mem) // wq
    qb = seq_len // bq
    return pl.pallas_call(
        functools.partial(_attn_kernel, bk=bk, unroll=unroll),
        grid=(batch, N_KV_HEADS, qb),
        in_specs=[pl.BlockSpec((bq, wq), lambda b, g, i: (b * qb + i, g)),
                  pl.BlockSpec((seq_len, HEAD_DIM), lambda b, g, i: (b, k_col + g)),
                  pl.BlockSpec((seq_len, HEAD_DIM), lambda b, g, i: (b, v_col + g)),
                  pl.BlockSpec((bq, wq), lambda b, g, i: (b * qb + i, g_col + g))],
        out_specs=pl.BlockSpec((bq, wq), lambda b, g, i: (b * qb + i, g)),
        out_shape=jax.ShapeDtypeStruct((T, att), BF16),
        scratch_shapes=[pltpu.VMEM((HEAD_DIM, seq_len), BF16),
                        pltpu.VMEM((seq_len, 2 * HEAD_DIM), BF16),
                        pltpu.VMEM((n_rep, bq, 2 * HEAD_DIM), F32),
                        pltpu.VMEM((n_rep, bq, HEAD_DIM), F32)],
        compiler_params=_params(("parallel", "parallel", "arbitrary")),
        name="gqa_attention",
    )(main, main, main, main)


def _matmul_kernel(x_ref, w_ref, o_ref):
    o_ref[...] = jnp.dot(x_ref[...].astype(BF16), w_ref[...],
                         preferred_element_type=F32).astype(o_ref.dtype)


def _matmul(x, w, bn):
    M, K = x.shape
    N = w.shape[1]
    return pl.pallas_call(
        _matmul_kernel,
        grid=(N // bn,),
        in_specs=[pl.BlockSpec((M, K), lambda n: (0, 0)),
                  pl.BlockSpec((K, bn), lambda n: (0, n))],
        out_specs=pl.BlockSpec((M, bn), lambda n: (0, n)),
        out_shape=jax.ShapeDtypeStruct((M, N), BF16),
        compiler_params=_params(("arbitrary",)),
        name="mem_kv_proj",
    )(x, w)


def _mem_attn_kernel(q_ref, k_ref, v_ref, g_ref, o_ref, *, scale):
    hd = q_ref.shape[1] // N_MEM_HEADS
    for h in range(N_MEM_HEADS):
        sl = slice(h * hd, (h + 1) * hd)
        q = (q_ref[:, sl].astype(F32) * scale).astype(BF16)
        s = lax.dot_general(q, k_ref[:, sl], (((1,), (1,)), ((), ())), preferred_element_type=F32)
        p = jnp.exp2(s - jnp.max(s, axis=-1, keepdims=True))
        l = jnp.sum(p, axis=-1, keepdims=True)
        o = jnp.dot(p.astype(BF16), v_ref[:, sl], preferred_element_type=F32)
        o_ref[:, sl] = (o / l * g_ref[:, sl].astype(F32)).astype(o_ref.dtype)


def _mem_attention(main, memkv, batch, seq_len, n_mem, widths, bl):
    att, kv, ssm, mem = widths
    T = main.shape[0]
    q_col = (att + 2 * kv) // mem
    g_col = (att + 2 * kv + mem + att + ssm) // mem
    lb = seq_len // bl
    scale = (mem // N_MEM_HEADS) ** -0.5 * LOG2E
    return pl.pallas_call(
        functools.partial(_mem_attn_kernel, scale=scale),
        grid=(batch, lb),
        in_specs=[pl.BlockSpec((bl, mem), lambda b, i: (b * lb + i, q_col)),
                  pl.BlockSpec((n_mem, mem), lambda b, i: (b, 0)),
                  pl.BlockSpec((n_mem, mem), lambda b, i: (b, 1)),
                  pl.BlockSpec((bl, mem), lambda b, i: (b * lb + i, g_col))],
        out_specs=pl.BlockSpec((bl, mem), lambda b, i: (b * lb + i, 0)),
        out_shape=jax.ShapeDtypeStruct((T, mem), BF16),
        compiler_params=_params(("parallel", "arbitrary")),
        name="mem_attention",
    )(main, memkv, memkv, main)


def _gelu_tanh(y):
    return 0.5 * y * (1.0 + jnp.tanh(math.sqrt(2.0 / math.pi) * (y + 0.044715 * (y * y * y))))


def _s5_kernel(*refs, reverse, tc):
    if reverse:
        u_ref, yin_ref, bm_ref, cm_ref, a_ref, y_ref, uc_ref, s_ref, st_ref = refs
    else:
        u_ref, tz_ref, bm_ref, cm_ref, a_ref, y_ref, uc_ref, s_ref, st_ref = refs
    rb = uc_ref.shape[0]
    ns = S5_UNIT_STATES

    @pl.when(pl.program_id(2) == 0)
    def _():
        st_ref[...] = jnp.zeros_like(st_ref)

    for j in range(tc):
        uc_ref[:, j * S5_UNIT:(j + 1) * S5_UNIT] = u_ref[pl.ds(j, rb, stride=tc), :].astype(BF16)
    uc = uc_ref[...]
    s_ref[...] = jnp.dot(uc, bm_ref[...], preferred_element_type=F32)
    ar = a_ref[0:1, :]
    ai = a_ref[1:2, :]

    def body(i, carry):
        xr, xi = carry
        c = (rb - 1 - i) if reverse else i
        sr = s_ref[pl.ds(c, 1), 0:ns]
        si = s_ref[pl.ds(c, 1), ns:2 * ns]
        s_ref[pl.ds(c, 1), 0:ns] = xr
        s_ref[pl.ds(c, 1), ns:2 * ns] = xi
        return ar * xr - ai * xi + sr, ar * xi + ai * xr + si

    xr, xi = lax.fori_loop(0, rb, body, (st_ref[0:1, :], st_ref[1:2, :]), unroll=8)
    st_ref[0:1, :] = xr
    st_ref[1:2, :] = xi

    y = jnp.dot(s_ref[...].astype(BF16), cm_ref[...], preferred_element_type=F32)
    if not reverse:
        y = y + jnp.dot(uc, tz_ref[...], preferred_element_type=F32)
    for t in range(tc):
        rows = pl.ds(t, rb, stride=tc)
        yt = y[:, t * S5_UNIT:(t + 1) * S5_UNIT]
        if reverse:
            y_ref[rows, :] = _gelu_tanh(yin_ref[rows, :] + yt)
        else:
            y_ref[rows, :] = yt


def _s5_pass(u, y_in, ops, batch, seq_len, reverse, rb):
    T, ssm = u.shape
    tc = S5_CHUNK
    n_units = ssm // S5_UNIT
    tok = rb * tc
    nb = seq_len // tok
    ns = S5_UNIT_STATES

    def row(b, r):
        return b * nb + ((nb - 1 - r) if reverse else r)

    tok_spec = pl.BlockSpec((tok, S5_UNIT), lambda n, b, r: (row(b, r), n))
    wspec = lambda shape: pl.BlockSpec((None,) + shape, lambda n, b, r: (n, 0, 0))
    if reverse:
        tz, bm, cm, a = None, ops["bmb"], ops["cmb"], ops["ab"]
        ins = [u, y_in, bm, cm, a]
        in_specs = [tok_spec, tok_spec]
    else:
        tz, bm, cm, a = ops["tz"], ops["bmf"], ops["cmf"], ops["af"]
        ins = [u, tz, bm, cm, a]
        in_specs = [tok_spec, wspec((tc * S5_UNIT, tc * S5_UNIT))]
    in_specs += [wspec((tc * S5_UNIT, 2 * ns)), wspec((2 * ns, tc * S5_UNIT)), wspec((2, ns))]
    return pl.pallas_call(
        functools.partial(_s5_kernel, reverse=reverse, tc=tc),
        grid=(n_units, batch, nb),
        in_specs=in_specs,
        out_specs=tok_spec,
        out_shape=jax.ShapeDtypeStruct((T, ssm), F32),
        scratch_shapes=[pltpu.VMEM((rb, tc * S5_UNIT), BF16),
                        pltpu.VMEM((rb, 2 * ns), F32),
                        pltpu.VMEM((2, ns), F32)],
        compiler_params=_params(("parallel", "arbitrary", "arbitrary")),
        name="s5_backward" if reverse else "s5_forward",
    )(*ins)


def _s5_operators(lam_re, lam_im, log_step, b_re, b_im, c_re, c_im, d_skip):
    hi = lax.Precision.HIGHEST
    tc = S5_CHUNK
    G = lam_re.shape[1]
    P, C = SSM_STATE, SSM_GROUP
    gu = S5_UNIT // C
    nu = G // gu
    lr = lam_re.astype(F32)
    li = lam_im.astype(F32)
    dt = jnp.exp(log_step.astype(F32))[..., None]
    mag = jnp.exp(lr * dt)
    ar = mag * jnp.cos(li * dt)
    ai = mag * jnp.sin(li * dt)
    den = lr * lr + li * li
    nr = ar - 1.0
    fr = (nr * lr + ai * li) / den
    fi = (ai * lr - nr * li) / den
    br = b_re.astype(F32)
    bi = b_im.astype(F32)
    bbar_r = fr[..., None] * br - fi[..., None] * bi
    bbar_i = fr[..., None] * bi + fi[..., None] * br
    e = jnp.arange(tc + 1, dtype=F32)
    pmag = jnp.exp(lr[..., None] * dt[..., None] * e)
    pang = li[..., None] * dt[..., None] * e
    pr = pmag * jnp.cos(pang)
    pi = pmag * jnp.sin(pang)
    cr = c_re.astype(F32)
    ci = c_im.astype(F32)
    eye = jnp.eye(gu, dtype=F32)
    ns, ch = gu * P, S5_UNIT

    def unit_c(c):
        return jnp.einsum('ngcp,gh->nhpgc', c.reshape(nu, gu, C, P), eye).reshape(nu, ns, ch)

    def unit_b(b):
        return jnp.einsum('ngpc,gh->ngchp', b.reshape(nu, gu, P, C), eye).reshape(nu, ch, ns)

    ctr, cti = unit_c(cr), unit_c(ci)
    pwr = pr.reshape(2, nu, ns, tc + 1)
    pwi = pi.reshape(2, nu, ns, tc + 1)
    ptr = jnp.transpose(pwr, (0, 3, 1, 2))
    pti = jnp.transpose(pwi, (0, 3, 1, 2))

    ops = {}
    lag = []
    for d in (0, 1):
        btr, bti = unit_b(bbar_r[d]), unit_b(bbar_i[d])
        pgr = jnp.moveaxis(pr[d][..., :tc], -1, 0)[..., None]
        pgi = jnp.moveaxis(pi[d][..., :tc], -1, 0)[..., None]
        ba_r = bbar_r[d][None] * pgr - bbar_i[d][None] * pgi
        ba_i = bbar_r[d][None] * pgi + bbar_i[d][None] * pgr
        kg = (jnp.einsum('tgpd,gcp->tgdc', ba_r, cr, precision=hi)
              - jnp.einsum('tgpd,gcp->tgdc', ba_i, ci, precision=hi))
        lag.append(jnp.einsum('tngdc,gh->tngdhc', kg.reshape(tc, nu, gu, C, C), eye).reshape(tc, nu, ch, ch))
        name = "fb"[d]
        ex = (tc - 1 - jnp.arange(tc)) if d == 0 else jnp.arange(tc)
        wr = jnp.transpose(ptr[d][ex], (1, 0, 2))[:, :, None, :]
        wi = jnp.transpose(pti[d][ex], (1, 0, 2))[:, :, None, :]
        bm = jnp.concatenate([(btr[:, None] * wr - bti[:, None] * wi).astype(BF16),
                              (btr[:, None] * wi + bti[:, None] * wr).astype(BF16)], axis=-1)
        ops["bm" + name] = bm.reshape(nu, tc * ch, 2 * ns)
        ex = [t + 1 if d == 0 else tc - t for t in range(tc)]
        cm_re = jnp.concatenate([(ctr * pwr[d][:, :, e:e + 1] - cti * pwi[d][:, :, e:e + 1]).astype(BF16)
                                 for e in ex], axis=-1)
        cm_im = jnp.concatenate([(-(ctr * pwi[d][:, :, e:e + 1] + cti * pwr[d][:, :, e:e + 1])).astype(BF16)
                                 for e in ex], axis=-1)
        ops["cm" + name] = jnp.concatenate([cm_re, cm_im], axis=1)
        ops["a" + name] = jnp.stack([pwr[d][:, :, tc], pwi[d][:, :, tc]], axis=1)
    kf, kb = lag
    dmat = d_skip.astype(F32).reshape(nu, ch)[:, :, None] * jnp.eye(ch, dtype=F32)[None]
    k0 = kf[0] + kb[0] + dmat
    lags = [kb[tc - 1 - i] for i in range(tc - 1)] + [k0] + [kf[i] for i in range(1, tc)]
    kcat = jnp.concatenate([k.astype(BF16) for k in lags], axis=-1)
    ops["tz"] = jnp.concatenate([kcat[:, :, (tc - 1 - j) * ch:(2 * tc - 1 - j) * ch] for j in range(tc)],
                                axis=1)
    return ops


def _glu_kernel(y_ref, w_ref, b_ref, g_ref, o_ref):
    y = y_ref[...]
    z = jnp.dot(y.astype(BF16), w_ref[...], preferred_element_type=F32) + b_ref[...]
    o_ref[...] = (y * jax.nn.sigmoid(z) * g_ref[...].astype(F32)).astype(o_ref.dtype)


def _glu(y, w, b, main, widths, bm):
    att, kv, ssm, mem = widths
    T = y.shape[0]
    g_col = (att + 2 * kv + mem + att) // ssm
    return pl.pallas_call(
        _glu_kernel,
        grid=(T // bm,),
        in_specs=[pl.BlockSpec((bm, ssm), lambda m: (m, 0)),
                  pl.BlockSpec((ssm, ssm), lambda m: (0, 0)),
                  pl.BlockSpec((1, ssm), lambda m: (0, 0)),
                  pl.BlockSpec((bm, ssm), lambda m: (m, g_col))],
        out_specs=pl.BlockSpec((bm, ssm), lambda m: (m, 0)),
        out_shape=jax.ShapeDtypeStruct((T, ssm), BF16),
        compiler_params=_params(("parallel",)),
        name="glu",
    )(y, w, b, main)


def _out_kernel(att_ref, ssm_ref, mem_ref, wa_ref, ws_ref, wm_ref, x_ref, g_ref, b_ref,
                o_ref, h_ref, *, alpha):
    n = pl.program_id(1)
    nn = pl.num_programs(1)
    h = (jnp.dot(att_ref[...], wa_ref[...], preferred_element_type=F32)
         + jnp.dot(ssm_ref[...], ws_ref[...], preferred_element_type=F32)
         + jnp.dot(mem_ref[...], wm_ref[...], preferred_element_type=F32))
    h_ref[n] = alpha * x_ref[...] + h

    @pl.when(n == nn - 1)
    def _():
        nt, bm, bn = h_ref.shape
        d = nt * bn
        tot = jnp.zeros((bm, 1), F32)
        for j in range(nt):
            tot = tot + jnp.sum(h_ref[j], axis=-1, keepdims=True)
        mu = tot / d
        sq = jnp.zeros((bm, 1), F32)
        for j in range(nt):
            hc = h_ref[j] - mu
            sq = sq + jnp.sum(hc * hc, axis=-1, keepdims=True)
        r = lax.rsqrt(sq / d + LN_EPS)
        for j in range(nt):
            sl = slice(j * bn, (j + 1) * bn)
            o_ref[:, sl] = (h_ref[j] - mu) * r * g_ref[:, sl] + b_ref[:, sl]


def _out_proj(att_o, ssm_o, mem_o, w_out, x, ln_g, ln_b, alpha, bm, bn):
    T, D = x.shape
    att, ssm, mem = att_o.shape[1], ssm_o.shape[1], mem_o.shape[1]
    return pl.pallas_call(
        functools.partial(_out_kernel, alpha=alpha),
        grid=(T // bm, D // bn),
        in_specs=[pl.BlockSpec((bm, att), lambda m, n: (m, 0)),
                  pl.BlockSpec((bm, ssm), lambda m, n: (m, 0)),
                  pl.BlockSpec((bm, mem), lambda m, n: (m, 0)),
                  pl.BlockSpec((att, bn), lambda m, n: (0, n)),
                  pl.BlockSpec((ssm, bn), lambda m, n: (att // ssm, n)),
                  pl.BlockSpec((mem, bn), lambda m, n: ((att + ssm) // mem, n)),
                  pl.BlockSpec((bm, bn), lambda m, n: (m, n)),
                  pl.BlockSpec((1, D), lambda m, n: (0, 0)),
                  pl.BlockSpec((1, D), lambda m, n: (0, 0))],
        out_specs=pl.BlockSpec((bm, D), lambda m, n: (m, 0)),
        out_shape=jax.ShapeDtypeStruct((T, D), F32),
        scratch_shapes=[pltpu.VMEM((D // bn, bm, bn), F32)],
        compiler_params=_params(("parallel", "arbitrary")),
        name="out_proj",
    )(att_o, ssm_o, mem_o, w_out, w_out, w_out, x, ln_g, ln_b)


def _rope_tables(seq_len, gain, scale):
    rows = seq_len // GRID_W
    row = jnp.broadcast_to(jnp.arange(rows, dtype=F32)[:, None], (rows, GRID_W)).reshape(seq_len)
    col = jnp.broadcast_to(jnp.arange(GRID_W, dtype=F32)[None, :], (rows, GRID_W)).reshape(seq_len)
    inv = ROPE_THETA ** (-jnp.arange(0, ROPE_SECTION, 2, dtype=F32) / ROPE_SECTION)
    ang_r = row[:, None] * inv[None, :]
    ang_c = col[:, None] * inv[None, :]
    ang = jnp.concatenate([ang_r, ang_r, ang_c, ang_c], axis=-1)
    cos, sin = jnp.cos(ang), jnp.sin(ang)
    half = ROPE_SECTION // 2
    first = (jnp.arange(HEAD_DIM) % ROPE_SECTION) < half
    g = gain.astype(F32) * scale
    c_t = cos * g[None, :]
    sa_t = jnp.where(first[None, :], -sin * jnp.roll(g, -half)[None, :], 0.0)
    sb_t = jnp.where(first[None, :], 0.0, sin * jnp.roll(g, half)[None, :])
    return c_t, sa_t, sb_t


def _pick(n, pref):
    b = min(n, pref)
    while n % b:
        b //= 2
    return b


IN_BN = 2 * N_KV_HEADS * HEAD_DIM
OUT_BN = 512


def INPROJ_BLOCKS(seq_len):
    return _pick(seq_len, 512), IN_BN


def OUTPROJ_BLOCKS(n_tokens):
    return _pick(n_tokens, 512), OUT_BN


def _layer(x3, mem3, w_in_p, q_tabs_fn, widths, ops, w_glu, b_glu, w_mem_kv, w_out, ln_g, ln_b, alpha):
    B, L, D = x3.shape
    att, kv, ssm, mem = widths
    x = x3.reshape(B * L, D)
    n_mem = mem3.shape[1]
    tables = q_tabs_fn(L)
    main, u = _in_proj(x, w_in_p, tables, L, widths, *INPROJ_BLOCKS(L))
    att_o = _attention(main, B, L, widths, *ATTN_BLOCKS(L))
    memkv = _matmul(mem3.reshape(B * n_mem, D), w_mem_kv, 512)
    mem_o = _mem_attention(main, memkv, B, L, n_mem, widths, _pick(L, 512))
    rb = _pick(L // S5_CHUNK, S5_ROW_BLOCK)
    y_f = _s5_pass(u, None, ops, B, L, False, rb)
    y = _s5_pass(u, y_f, ops, B, L, True, rb)
    ssm_o = _glu(y, w_glu, b_glu, main, widths, _pick(B * L, 512))
    out = _out_proj(att_o, ssm_o, mem_o, w_out, x, ln_g, ln_b, alpha, *OUTPROJ_BLOCKS(B * L))
    return out.reshape(B, L, D)


def kernel(x_prompt, x_sample, mem_prompt, mem_sample, w_in, q_norm_g, k_norm_g, ssm_lam_re, ssm_lam_im, ssm_log_step, ssm_b_re, ssm_b_im, ssm_c_re, ssm_c_im, ssm_d, w_glu, b_glu, w_mem_kv, w_out, ln_g, ln_b):
    depth, d_model, in_width = w_in.shape
    att = d_model // 2
    kv = N_KV_HEADS * HEAD_DIM
    ssm = d_model // 4
    mem = d_model - att - ssm
    widths = (att, kv, ssm, mem)
    alpha = (2 * depth) ** 0.25
    o = [0, att, att + kv, att + 2 * kv, 2 * att + 2 * kv, 2 * att + 2 * kv + ssm,
         2 * att + 2 * kv + 2 * ssm, 2 * att + 2 * kv + 2 * ssm + mem, in_width]
    seg = lambda i: (o[i], o[i + 1])
    order = [seg(0), seg(1), seg(2), seg(6), seg(3), seg(5), seg(7), seg(4)]

    y_p, y_s = x_prompt, x_sample
    for l in range(depth):
        w_in_p = jnp.concatenate([w_in[l][:, a:b] for a, b in order], axis=1).astype(BF16)
        att_scale = HEAD_DIM ** -0.5 * LOG2E

        def tabs(L, l=l):
            return (_rope_tables(L, q_norm_g[l], att_scale) + _rope_tables(L, k_norm_g[l], 1.0))

        ops = _s5_operators(ssm_lam_re[l], ssm_lam_im[l], ssm_log_step[l], ssm_b_re[l], ssm_b_im[l],
                            ssm_c_re[l], ssm_c_im[l], ssm_d[l])
        args = (w_in_p, tabs, widths, ops, w_glu[l].astype(BF16), b_glu[l].astype(F32)[None, :],
                w_mem_kv[l].astype(BF16), w_out[l].astype(BF16),
                ln_g[l].astype(F32)[None, :], ln_b[l].astype(F32)[None, :], alpha)
        y_p = _layer(y_p, mem_prompt, *args)
        y_s = _layer(y_s, mem_sample, *args)
    return (y_p, y_s)
```

```python
import functools
import math

import jax
import jax.numpy as jnp
from jax import lax
from jax.experimental import pallas as pl
from jax.experimental.pallas import tpu as pltpu

F32 = jnp.float32
BF16 = jnp.bfloat16

HEAD_DIM = 128
N_KV_HEADS = 4
GRID_W = 64
ROPE_SECTION = HEAD_DIM // 2
ROPE_THETA = 10000.0
RMS_EPS = 1e-6
LN_EPS = 1e-5
SSM_GROUP = 16
SSM_STATE = 64
N_MEM_HEADS = 4
LOG2E = 1.4426950408889634

V7X_LANES = 128
V7X_VMEM_BYTES = 64 * 1024 * 1024
VMEM_LIMIT = V7X_VMEM_BYTES - 6 * 1024 * 1024

S5_CHUNK = 16
S5_UNIT = V7X_LANES
S5_UNIT_STATES = (S5_UNIT // SSM_GROUP) * SSM_STATE
S5_ROW_BLOCK = 256


def _params(sem):
    return pltpu.CompilerParams(dimension_semantics=sem, vmem_limit_bytes=VMEM_LIMIT)


def _silu(x):
    h = 0.5 * x
    return h + h * jnp.tanh(h)


def _in_proj_kernel(x_ref, w_ref, cq_ref, saq_ref, sbq_ref, ck_ref, sak_ref, sbk_ref,
                    o_ref, u_ref, xb_ref, acc_ref, *, n_q, n_qk, n_plain, n_main, n_tiles, kv_heads):
    g = pl.program_id(0)
    last = pl.num_programs(0) - 1
    n = g % n_tiles
    cur, prv = g % 2, (g + 1) % 2

    def matmul():
        acc_ref[cur] = jnp.dot(xb_ref[...], w_ref[...], preferred_element_type=F32)

    def cast_rows():
        xb_ref[...] = x_ref[...].astype(BF16)

    def norm_rope(cos_ref, sa_ref, sb_ref, heads):
        for h in range(heads):
            t = acc_ref[prv, :, h * HEAD_DIM:(h + 1) * HEAD_DIM]
            r = lax.rsqrt(jnp.mean(t * t, axis=-1, keepdims=True) + RMS_EPS)
            y = (t * cos_ref[...]
                 + pltpu.roll(t, HEAD_DIM - ROPE_SECTION // 2, 1) * sa_ref[...]
                 + pltpu.roll(t, ROPE_SECTION // 2, 1) * sb_ref[...])
            o_ref[:, h * HEAD_DIM:(h + 1) * HEAD_DIM] = (y * r).astype(o_ref.dtype)

    def rope_q():
        norm_rope(cq_ref, saq_ref, sbq_ref, o_ref.shape[1] // HEAD_DIM)

    def rope_k_and_v():
        norm_rope(ck_ref, sak_ref, sbk_ref, kv_heads)
        o_ref[:, kv_heads * HEAD_DIM:] = acc_ref[prv, :, kv_heads * HEAD_DIM:].astype(o_ref.dtype)

    def plain():
        o_ref[...] = acc_ref[prv].astype(o_ref.dtype)

    def silu():
        o_ref[...] = _silu(acc_ref[prv]).astype(o_ref.dtype)

    def ssm_input():
        u_ref[...] = acc_ref[prv]

    @pl.when(g == 0)
    def _():
        cast_rows()
        matmul()

    @pl.when((n == 0) & (g > 0) & (g < last))
    def _():
        ssm_input()
        cast_rows()
        matmul()

    @pl.when(g == last)
    def _():
        ssm_input()

    bounds = [(0, n_q, rope_q), (n_q, n_qk, rope_k_and_v), (n_qk, n_plain, plain),
              (n_plain, n_main, silu), (n_main, n_tiles, ssm_input)]
    for lo, hi, epilogue in bounds:
        hi = min(hi, n_tiles - 1)

        @pl.when((n > lo) & (n <= hi) & (g < last))
        def _(epilogue=epilogue):
            epilogue()
            matmul()


def _in_proj(x, w, tables, seq_len, widths, bm, bn):
    T, D = x.shape
    att, kv, ssm, mem = widths
    assert bn == 2 * kv and att % bn == 0 and ssm % bn == 0 and mem % bn == 0
    n_q = att // bn
    n_qk = n_q + 1
    n_plain = n_qk + mem // bn
    n_main = n_plain + (att + ssm + mem) // bn
    n_u = ssm // bn
    n_tiles = n_main + n_u
    w_main = n_main * bn
    row_blocks = T // bm
    pos_blocks = seq_len // bm
    steps = row_blocks * n_tiles + 1

    def row(g):
        return jnp.minimum(g // n_tiles, row_blocks - 1)

    def done(g):
        gp = jnp.maximum(g - 1, 0)
        return gp // n_tiles, gp % n_tiles

    tab_spec = pl.BlockSpec((bm, HEAD_DIM), lambda g: (row(g) % pos_blocks, 0))
    kern = functools.partial(_in_proj_kernel, n_q=n_q, n_qk=n_qk, n_plain=n_plain, n_main=n_main,
                             n_tiles=n_tiles, kv_heads=kv // HEAD_DIM)
    return pl.pallas_call(
        kern,
        grid=(steps,),
        in_specs=[pl.BlockSpec((bm, D), lambda g: (row(g), 0)),
                  pl.BlockSpec((D, bn), lambda g: (0, jnp.where(g == steps - 1, n_tiles - 1, g % n_tiles)))]
                 + [tab_spec] * 6,
        out_specs=[pl.BlockSpec((bm, bn), lambda g: (done(g)[0], jnp.minimum(done(g)[1], n_main - 1))),
                   pl.BlockSpec((bm, bn), lambda g: (done(g)[0], jnp.clip(done(g)[1] - n_main, 0, n_u - 1)))],
        out_shape=[jax.ShapeDtypeStruct((T, w_main), BF16),
                   jax.ShapeDtypeStruct((T, ssm), F32)],
        scratch_shapes=[pltpu.VMEM((bm, D), BF16), pltpu.VMEM((2, bm, bn), F32)],
        compiler_params=_params(("arbitrary",)),
        name="in_proj",
    )(x, w, *tables)


def _attn_kernel(q_ref, k_ref, v_ref, g_ref, o_ref, kt_ref, vp_ref, acc_ref, m_ref, *, bk, unroll):
    bq = q_ref.shape[0]
    n_rep = q_ref.shape[1] // HEAD_DIM
    seq = k_ref.shape[0]
    nk = seq // bk

    @pl.when(pl.program_id(2) == 0)
    def _():
        for c in range(seq // bk):
            kt_ref[:, c * bk:(c + 1) * bk] = k_ref[c * bk:(c + 1) * bk, :].T
        vp_ref[:, :HEAD_DIM] = v_ref[...]
        vp_ref[:, HEAD_DIM:] = jnp.ones((seq, HEAD_DIM), vp_ref.dtype)

    acc_ref[...] = jnp.zeros_like(acc_ref)
    m_ref[...] = jnp.full(m_ref.shape, -1e30, F32)

    def body(j, carry):
        off = pl.multiple_of(j * bk, bk)
        kt = kt_ref[:, pl.ds(off, bk)]
        vp = vp_ref[pl.ds(off, bk), :]
        for h in range(n_rep):
            q = q_ref[:, h * HEAD_DIM:(h + 1) * HEAD_DIM]
            s = jnp.dot(q, kt, preferred_element_type=F32)
            m_old = m_ref[h]
            m_new = jnp.maximum(m_old, jnp.max(s, axis=-1, keepdims=True))
            p = jnp.exp2(s - jnp.concatenate([m_new] * (bk // HEAD_DIM), axis=1)).astype(BF16)
            alpha = jnp.exp2(m_old - m_new)
            acc_ref[h] = (jnp.concatenate([alpha, alpha], axis=1) * acc_ref[h]
                          + jnp.dot(p, vp, preferred_element_type=F32))
            m_ref[h] = m_new
        return carry

    lax.fori_loop(0, nk, body, 0, unroll=unroll)
    for h in range(n_rep):
        a = acc_ref[h]
        sl = slice(h * HEAD_DIM, (h + 1) * HEAD_DIM)
        o_ref[:, sl] = (a[:, :HEAD_DIM] / a[:, HEAD_DIM:] * g_ref[:, sl].astype(F32)).astype(o_ref.dtype)


def ATTN_BLOCKS(seq_len):
    bk = _pick(seq_len, 512)
    return _pick(seq_len, 1024), bk, math.gcd(seq_len // bk, 4)


def _attention(main, batch, seq_len, widths, bq, bk, unroll):
    att, kv, ssm, mem = widths
    T = main.shape[0]
    n_rep = att // kv
    wq = n_rep * HEAD_DIM
    k_col = att // HEAD_DIM
    v_col = (att + kv) // HEAD_DIM
    g_col = (att + 2 * kv + mem) // wq
    qb = seq_len // bq
    return pl.pallas_call(
        functools.partial(_attn_kernel, bk=bk, unroll=unroll),
        grid=(batch, N_KV_HEADS, qb),
        in_specs=[pl.BlockSpec((bq, wq), lambda b, g, i: (b * qb + i, g)),
                  pl.BlockSpec((seq_len, HEAD_DIM), lambda b, g, i: (b, k_col + g)),
                  pl.BlockSpec((seq_len, HEAD_DIM), lambda b, g, i: (b, v_col + g)),
                  pl.BlockSpec((bq, wq), lambda b, g, i: (b * qb + i, g_col + g))],
        out_specs=pl.BlockSpec((bq, wq), lambda b, g, i: (b * qb + i, g)),
        out_shape=jax.ShapeDtypeStruct((T, att), BF16),
        scratch_shapes=[pltpu.VMEM((HEAD_DIM, seq_len), BF16),
                        pltpu.VMEM((seq_len, 2 * HEAD_DIM), BF16),
                        pltpu.VMEM((n_rep, bq, 2 * HEAD_DIM), F32),
                        pltpu.VMEM((n_rep, bq, HEAD_DIM), F32)],
        compiler_params=_params(("parallel", "parallel", "arbitrary")),
        name="gqa_attention",
    )(main, main, main, main)


def _matmul_kernel(x_ref, w_ref, o_ref):
    o_ref[...] = jnp.dot(x_ref[...].astype(BF16), w_ref[...],
                         preferred_element_type=F32).astype(o_ref.dtype)


def _matmul(x, w, bn):
    M, K = x.shape
    N = w.shape[1]
    return pl.pallas_call(
        _matmul_kernel,
        grid=(N // bn,),
        in_specs=[pl.BlockSpec((M, K), lambda n: (0, 0)),
                  pl.BlockSpec((K, bn), lambda n: (0, n))],
        out_specs=pl.BlockSpec((M, bn), lambda n: (0, n)),
        out_shape=jax.ShapeDtypeStruct((M, N), BF16),
        compiler_params=_params(("arbitrary",)),
        name="mem_kv_proj",
    )(x, w)


def _mem_attn_kernel(q_ref, k_ref, v_ref, g_ref, o_ref, *, scale):
    hd = q_ref.shape[1] // N_MEM_HEADS
    for h in range(N_MEM_HEADS):
        sl = slice(h * hd, (h + 1) * hd)
        q = (q_ref[:, sl].astype(F32) * scale).astype(BF16)
        s = lax.dot_general(q, k_ref[:, sl], (((1,), (1,)), ((), ())), preferred_element_type=F32)
        p = jnp.exp2(s - jnp.max(s, axis=-1, keepdims=True))
        l = jnp.sum(p, axis=-1, keepdims=True)
        o = jnp.dot(p.astype(BF16), v_ref[:, sl], preferred_element_type=F32)
        o_ref[:, sl] = (o / l * g_ref[:, sl].astype(F32)).astype(o_ref.dtype)


def _mem_attention(main, memkv, batch, seq_len, n_mem, widths, bl):
    att, kv, ssm, mem = widths
    T = main.shape[0]
    q_col = (att + 2 * kv) // mem
    g_col = (att + 2 * kv + mem + att + ssm) // mem
    lb = seq_len // bl
    scale = (mem // N_MEM_HEADS) ** -0.5 * LOG2E
    return pl.pallas_call(
        functools.partial(_mem_attn_kernel, scale=scale),
        grid=(batch, lb),
        in_specs=[pl.BlockSpec((bl, mem), lambda b, i: (b * lb + i, q_col)),
                  pl.BlockSpec((n_mem, mem), lambda b, i: (b, 0)),
                  pl.BlockSpec((n_mem, mem), lambda b, i: (b, 1)),
                  pl.BlockSpec((bl, mem), lambda b, i: (b * lb + i, g_col))],
        out_specs=pl.BlockSpec((bl, mem), lambda b, i: (b * lb + i, 0)),
        out_shape=jax.ShapeDtypeStruct((T, mem), BF16),
        compiler_params=_params(("parallel", "arbitrary")),
        name="mem_attention",
    )(main, memkv, memkv, main)


def _gelu_tanh(y):
    return 0.5 * y * (1.0 + jnp.tanh(math.sqrt(2.0 / math.pi) * (y + 0.044715 * (y * y * y))))


def _s5_kernel(*refs, reverse, tc):
    if reverse:
        u_ref, yin_ref, bt_ref, ct_ref, wt_ref, et_ref, a_ref, y_ref, bm_ref, cm_ref, uc_ref, s_ref, st_ref = refs
    else:
        (u_ref, kcat_ref, bt_ref, ct_ref, wt_ref, et_ref, a_ref, y_ref,
         tz_ref, bm_ref, cm_ref, uc_ref, s_ref, st_ref) = refs
    rb = uc_ref.shape[0]
    ns = S5_UNIT_STATES
    ch = S5_UNIT

    @pl.when((pl.program_id(1) == 0) & (pl.program_id(2) == 0))
    def _():
        def scaled(f_ref, tab_ref, out_ref, sign):
            fr, fi = f_ref[0], f_ref[1]
            for j in range(tc):
                wr, wi = tab_ref[0, j:j + 1, :], tab_ref[1, j:j + 1, :]
                out_ref[j * ch:(j + 1) * ch, 0:ns] = (fr * wr - fi * wi).astype(BF16)
                out_ref[j * ch:(j + 1) * ch, ns:2 * ns] = (sign * (fr * wi + fi * wr)).astype(BF16)

        scaled(bt_ref, wt_ref, bm_ref, 1.0)
        scaled(ct_ref, et_ref, cm_ref, -1.0)
        if not reverse:
            for j in range(tc):
                tz_ref[j * ch:(j + 1) * ch, :] = kcat_ref[:, (tc - 1 - j) * ch:(2 * tc - 1 - j) * ch]

    @pl.when(pl.program_id(2) == 0)
    def _():
        st_ref[...] = jnp.zeros_like(st_ref)

    for j in range(tc):
        uc_ref[:, j * ch:(j + 1) * ch] = u_ref[pl.ds(j, rb, stride=tc), :].astype(BF16)
    uc = uc_ref[...]
    s_ref[...] = jnp.dot(uc, bm_ref[...], preferred_element_type=F32)
    ar = a_ref[0:1, :]
    ai = a_ref[1:2, :]

    def body(i, carry):
        xr, xi = carry
        c = (rb - 1 - i) if reverse else i
        sr = s_ref[pl.ds(c, 1), 0:ns]
        si = s_ref[pl.ds(c, 1), ns:2 * ns]
        s_ref[pl.ds(c, 1), 0:ns] = xr
        s_ref[pl.ds(c, 1), ns:2 * ns] = xi
        return ar * xr - ai * xi + sr, ar * xi + ai * xr + si

    xr, xi = lax.fori_loop(0, rb, body, (st_ref[0:1, :], st_ref[1:2, :]), unroll=8)
    st_ref[0:1, :] = xr
    st_ref[1:2, :] = xi

    y = lax.dot_general(s_ref[...].astype(BF16), cm_ref[...], (((1,), (1,)), ((), ())),
                        preferred_element_type=F32)
    if not reverse:
        y = y + jnp.dot(uc, tz_ref[...], preferred_element_type=F32)
    for t in range(tc):
        rows = pl.ds(t, rb, stride=tc)
        yt = y[:, t * ch:(t + 1) * ch]
        if reverse:
            y_ref[rows, :] = _gelu_tanh(yin_ref[rows, :] + yt)
        else:
            y_ref[rows, :] = yt


def _s5_pass(u, y_in, ops, batch, seq_len, reverse, rb):
    T, ssm = u.shape
    tc = S5_CHUNK
    n_units = ssm // S5_UNIT
    tok = rb * tc
    nb = seq_len // tok
    ns = S5_UNIT_STATES
    ch = S5_UNIT

    def row(b, r):
        return b * nb + ((nb - 1 - r) if reverse else r)

    tok_spec = pl.BlockSpec((tok, ch), lambda n, b, r: (row(b, r), n))

    def unit_spec(*shape):
        return pl.BlockSpec((None,) + shape, lambda n, b, r: (n,) + (0,) * len(shape))

    d = "b" if reverse else "f"
    factors = [ops["bt" + d], ops["ct"], ops["wt" + d], ops["et" + d], ops["a" + d]]
    factor_specs = [unit_spec(2, ch, ns), unit_spec(2, ch, ns), unit_spec(2, tc, ns), unit_spec(2, tc, ns),
                    unit_spec(2, ns)]
    op_scratch = [pltpu.VMEM((tc * ch, 2 * ns), BF16), pltpu.VMEM((tc * ch, 2 * ns), BF16)]
    if reverse:
        ins, in_specs = [u, y_in], [tok_spec, tok_spec]
    else:
        ins, in_specs = [u, ops["kcat"]], [tok_spec, unit_spec(ch, (2 * tc - 1) * ch)]
        op_scratch = [pltpu.VMEM((tc * ch, tc * ch), BF16)] + op_scratch
    return pl.pallas_call(
        functools.partial(_s5_kernel, reverse=reverse, tc=tc),
        grid=(n_units, batch, nb),
        in_specs=in_specs + factor_specs,
        out_specs=tok_spec,
        out_shape=jax.ShapeDtypeStruct((T, ssm), F32),
        scratch_shapes=op_scratch + [pltpu.VMEM((rb, tc * ch), BF16),
                                     pltpu.VMEM((rb, 2 * ns), F32),
                                     pltpu.VMEM((2, ns), F32)],
        compiler_params=_params(("parallel", "arbitrary", "arbitrary")),
        name="s5_backward" if reverse else "s5_forward",
    )(*ins, *factors)


def _s5_operators(lam_re, lam_im, log_step, b_re, b_im, c_re, c_im, d_skip):
    hi = lax.Precision.HIGHEST
    tc = S5_CHUNK
    G = lam_re.shape[1]
    P, C = SSM_STATE, SSM_GROUP
    gu = S5_UNIT // C
    nu = G // gu
    ns, ch = gu * P, S5_UNIT
    lr = lam_re.astype(F32)
    li = lam_im.astype(F32)
    dt = jnp.exp(log_step.astype(F32))[..., None]
    mag = jnp.exp(lr * dt)
    ar = mag * jnp.cos(li * dt)
    ai = mag * jnp.sin(li * dt)
    den = lr * lr + li * li
    nr = ar - 1.0
    fr = (nr * lr + ai * li) / den
    fi = (ai * lr - nr * li) / den
    br = b_re.astype(F32)
    bi = b_im.astype(F32)
    bbar_r = fr[..., None] * br - fi[..., None] * bi
    bbar_i = fr[..., None] * bi + fi[..., None] * br
    e = jnp.arange(tc + 1, dtype=F32)[None, :, None, None]
    pmag = jnp.exp(lr[:, None] * dt[:, None] * e)
    pang = li[:, None] * dt[:, None] * e
    pr = pmag * jnp.cos(pang)
    pi = pmag * jnp.sin(pang)
    cr = c_re.astype(F32)
    ci = c_im.astype(F32)
    eye = jnp.eye(gu, dtype=F32)

    def unit_cs(m):
        return jnp.einsum('ngcp,gh->ngchp', m.reshape(nu, gu, C, P), eye).reshape(nu, ch, ns)

    def unit_table(t):
        return jnp.transpose(t.reshape(t.shape[0], nu, ns), (1, 0, 2))

    ops = {"ct": jnp.stack([unit_cs(cr), unit_cs(ci)], axis=1)}
    lag = []
    for d, name in ((0, "f"), (1, "b")):
        ops["bt" + name] = jnp.stack([unit_cs(jnp.swapaxes(bbar_r[d], 1, 2)),
                                      unit_cs(jnp.swapaxes(bbar_i[d], 1, 2))], axis=1)
        ex = (tc - 1 - jnp.arange(tc)) if d == 0 else jnp.arange(tc)
        ops["wt" + name] = jnp.stack([unit_table(pr[d][ex]), unit_table(pi[d][ex])], axis=1)
        ex = (jnp.arange(tc) + 1) if d == 0 else (tc - jnp.arange(tc))
        ops["et" + name] = jnp.stack([unit_table(pr[d][ex]), unit_table(pi[d][ex])], axis=1)
        ops["a" + name] = jnp.stack([pr[d][tc].reshape(nu, ns), pi[d][tc].reshape(nu, ns)], axis=1)
        pgr = pr[d][:tc][..., None]
        pgi = pi[d][:tc][..., None]
        ba_r = bbar_r[d][None] * pgr - bbar_i[d][None] * pgi
        ba_i = bbar_r[d][None] * pgi + bbar_i[d][None] * pgr
        lag.append(jnp.einsum('tgpd,gcp->tgdc', ba_r, cr, precision=hi)
                   - jnp.einsum('tgpd,gcp->tgdc', ba_i, ci, precision=hi))
    kf, kb = lag
    k0 = kf[0] + kb[0] + d_skip.astype(F32).reshape(G, C)[:, :, None] * jnp.eye(C, dtype=F32)[None]
    kall = jnp.concatenate([kb[1:][::-1], k0[None], kf[1:]], axis=0)
    kcat = jnp.einsum('lngdc,gh->ngdlhc', kall.reshape(2 * tc - 1, nu, gu, C, C), eye)
    ops["kcat"] = kcat.reshape(nu, ch, (2 * tc - 1) * ch).astype(BF16)
    return ops


def _glu_kernel(y_ref, w_ref, b_ref, g_ref, o_ref):
    y = y_ref[...]
    z = jnp.dot(y.astype(BF16), w_ref[...], preferred_element_type=F32) + b_ref[...]
    o_ref[...] = (y * jax.nn.sigmoid(z) * g_ref[...].astype(F32)).astype(o_ref.dtype)


def _glu(y, w, b, main, widths, bm):
    att, kv, ssm, mem = widths
    T = y.shape[0]
    g_col = (att + 2 * kv + mem + att) // ssm
    return pl.pallas_call(
        _glu_kernel,
        grid=(T // bm,),
        in_specs=[pl.BlockSpec((bm, ssm), lambda m: (m, 0)),
                  pl.BlockSpec((ssm, ssm), lambda m: (0, 0)),
                  pl.BlockSpec((1, ssm), lambda m: (0, 0)),
                  pl.BlockSpec((bm, ssm), lambda m: (m, g_col))],
        out_specs=pl.BlockSpec((bm, ssm), lambda m: (m, 0)),
        out_shape=jax.ShapeDtypeStruct((T, ssm), BF16),
        compiler_params=_params(("parallel",)),
        name="glu",
    )(y, w, b, main)


def _out_kernel(att_ref, ssm_ref, mem_ref, wa_ref, ws_ref, wm_ref, x_ref, g_ref, b_ref,
                o_ref, *, alpha, n_tiles):
    n = pl.program_id(1)
    bm, d = o_ref.shape
    bn = wa_ref.shape[1]

    for j in range(n_tiles):
        @pl.when(n == j)
        def _(j=j):
            h = (jnp.dot(att_ref[...], wa_ref[...], preferred_element_type=F32)
                 + jnp.dot(ssm_ref[...], ws_ref[...], preferred_element_type=F32)
                 + jnp.dot(mem_ref[...], wm_ref[...], preferred_element_type=F32))
            o_ref[:, j * bn:(j + 1) * bn] = alpha * x_ref[...] + h

    @pl.when(n == n_tiles - 1)
    def _():
        cols = [slice(c, c + LN_CHUNK) for c in range(0, d, LN_CHUNK)]
        tot = jnp.zeros((bm, 1), F32)
        for sl in cols:
            tot = tot + jnp.sum(o_ref[:, sl], axis=-1, keepdims=True)
        mu = tot / d
        sq = jnp.zeros((bm, 1), F32)
        for sl in cols:
            hc = o_ref[:, sl] - mu
            sq = sq + jnp.sum(hc * hc, axis=-1, keepdims=True)
        r = lax.rsqrt(sq / d + LN_EPS)
        for sl in cols:
            o_ref[:, sl] = (o_ref[:, sl] - mu) * r * g_ref[:, sl] + b_ref[:, sl]


def _out_proj(att_o, ssm_o, mem_o, w_out, x, ln_g, ln_b, alpha, bm, bn):
    T, D = x.shape
    att, ssm, mem = att_o.shape[1], ssm_o.shape[1], mem_o.shape[1]
    return pl.pallas_call(
        functools.partial(_out_kernel, alpha=alpha, n_tiles=D // bn),
        grid=(T // bm, D // bn),
        in_specs=[pl.BlockSpec((bm, att), lambda m, n: (m, 0)),
                  pl.BlockSpec((bm, ssm), lambda m, n: (m, 0)),
                  pl.BlockSpec((bm, mem), lambda m, n: (m, 0)),
                  pl.BlockSpec((att, bn), lambda m, n: (0, n)),
                  pl.BlockSpec((ssm, bn), lambda m, n: (att // ssm, n)),
                  pl.BlockSpec((mem, bn), lambda m, n: ((att + ssm) // mem, n)),
                  pl.BlockSpec((bm, bn), lambda m, n: (m, n)),
                  pl.BlockSpec((1, D), lambda m, n: (0, 0)),
                  pl.BlockSpec((1, D), lambda m, n: (0, 0))],
        out_specs=pl.BlockSpec((bm, D), lambda m, n: (m, 0)),
        out_shape=jax.ShapeDtypeStruct((T, D), F32),
        compiler_params=_params(("parallel", "arbitrary")),
        name="out_proj",
    )(att_o, ssm_o, mem_o, w_out, w_out, w_out, x, ln_g, ln_b)


def _rope_tables(seq_len, gain, scale):
    rows = seq_len // GRID_W
    row = jnp.broadcast_to(jnp.arange(rows, dtype=F32)[:, None], (rows, GRID_W)).reshape(seq_len)
    col = jnp.broadcast_to(jnp.arange(GRID_W, dtype=F32)[None, :], (rows, GRID_W)).reshape(seq_len)
    inv = ROPE_THETA ** (-jnp.arange(0, ROPE_SECTION, 2, dtype=F32) / ROPE_SECTION)
    ang_r = row[:, None] * inv[None, :]
    ang_c = col[:, None] * inv[None, :]
    ang = jnp.concatenate([ang_r, ang_r, ang_c, ang_c], axis=-1)
    cos, sin = jnp.cos(ang), jnp.sin(ang)
    half = ROPE_SECTION // 2
    first = (jnp.arange(HEAD_DIM) % ROPE_SECTION) < half
    g = gain.astype(F32) * scale
    c_t = cos * g[None, :]
    sa_t = jnp.where(first[None, :], -sin * jnp.roll(g, -half)[None, :], 0.0)
    sb_t = jnp.where(first[None, :], 0.0, sin * jnp.roll(g, half)[None, :])
    return c_t, sa_t, sb_t


def _pick(n, pref):
    b = min(n, pref)
    while n % b:
        b //= 2
    return b


IN_BN = 2 * N_KV_HEADS * HEAD_DIM
OUT_BN = 1024
LN_CHUNK = 512


def INPROJ_BLOCKS(seq_len):
    return _pick(seq_len, 512), IN_BN


def OUTPROJ_BLOCKS(n_tokens):
    return _pick(n_tokens, 512), OUT_BN


def _layer(x3, mem3, w_in_p, q_tabs_fn, widths, ops, w_glu, b_glu, w_mem_kv, w_out, ln_g, ln_b, alpha):
    B, L, D = x3.shape
    att, kv, ssm, mem = widths
    x = x3.reshape(B * L, D)
    n_mem = mem3.shape[1]
    tables = q_tabs_fn(L)
    main, u = _in_proj(x, w_in_p, tables, L, widths, *INPROJ_BLOCKS(L))
    att_o = _attention(main, B, L, widths, *ATTN_BLOCKS(L))
    memkv = _matmul(mem3.reshape(B * n_mem, D), w_mem_kv, 512)
    mem_o = _mem_attention(main, memkv, B, L, n_mem, widths, _pick(L, 512))
    rb = _pick(L // S5_CHUNK, S5_ROW_BLOCK)
    y_f = _s5_pass(u, None, ops, B, L, False, rb)
    y = _s5_pass(u, y_f, ops, B, L, True, rb)
    ssm_o = _glu(y, w_glu, b_glu, main, widths, _pick(B * L, 512))
    out = _out_proj(att_o, ssm_o, mem_o, w_out, x, ln_g, ln_b, alpha, *OUTPROJ_BLOCKS(B * L))
    return out.reshape(B, L, D)


def kernel(x_prompt, x_sample, mem_prompt, mem_sample, w_in, q_norm_g, k_norm_g, ssm_lam_re, ssm_lam_im, ssm_log_step, ssm_b_re, ssm_b_im, ssm_c_re, ssm_c_im, ssm_d, w_glu, b_glu, w_mem_kv, w_out, ln_g, ln_b):
    depth, d_model, in_width = w_in.shape
    att = d_model // 2
    kv = N_KV_HEADS * HEAD_DIM
    ssm = d_model // 4
    mem = d_model - att - ssm
    widths = (att, kv, ssm, mem)
    alpha = (2 * depth) ** 0.25
    o = [0, att, att + kv, att + 2 * kv, 2 * att + 2 * kv, 2 * att + 2 * kv + ssm,
         2 * att + 2 * kv + 2 * ssm, 2 * att + 2 * kv + 2 * ssm + mem, in_width]
    seg = lambda i: (o[i], o[i + 1])
    order = [seg(0), seg(1), seg(2), seg(6), seg(3), seg(5), seg(7), seg(4)]

    y_p, y_s = x_prompt, x_sample
    for l in range(depth):
        w_in_p = jnp.concatenate([w_in[l][:, a:b] for a, b in order], axis=1).astype(BF16)
        att_scale = HEAD_DIM ** -0.5 * LOG2E

        def tabs(L, l=l):
            return (_rope_tables(L, q_norm_g[l], att_scale) + _rope_tables(L, k_norm_g[l], 1.0))

        ops = _s5_operators(ssm_lam_re[l], ssm_lam_im[l], ssm_log_step[l], ssm_b_re[l], ssm_b_im[l],
                            ssm_c_re[l], ssm_c_im[l], ssm_d[l])
        args = (w_in_p, tabs, widths, ops, w_glu[l].astype(BF16), b_glu[l].astype(F32)[None, :],
                w_mem_kv[l].astype(BF16), w_out[l].astype(BF16),
                ln_g[l].astype(F32)[None, :], ln_b[l].astype(F32)[None, :], alpha)
        y_p = _layer(y_p, mem_prompt, *args)
        y_s = _layer(y_s, mem_sample, *args)
    return (y_p, y_s)
```

```python
import functools
import math

import jax
import jax.numpy as jnp
from jax import lax
from jax.experimental import pallas as pl
from jax.experimental.pallas import tpu as pltpu

F32 = jnp.float32
BF16 = jnp.bfloat16

HEAD_DIM = 128
N_KV_HEADS = 4
GRID_W = 64
ROPE_SECTION = HEAD_DIM // 2
ROPE_THETA = 10000.0
RMS_EPS = 1e-6
LN_EPS = 1e-5
SSM_GROUP = 16
SSM_STATE = 64
N_MEM_HEADS = 4
LOG2E = 1.4426950408889634

V7X_LANES = 128
V7X_VMEM_BYTES = 64 * 1024 * 1024
VMEM_LIMIT = V7X_VMEM_BYTES - 6 * 1024 * 1024

S5_CHUNK = 16
S5_UNIT = V7X_LANES
S5_UNIT_STATES = (S5_UNIT // SSM_GROUP) * SSM_STATE
S5_ROW_BLOCK = 256


def _params(sem):
    return pltpu.CompilerParams(dimension_semantics=sem, vmem_limit_bytes=VMEM_LIMIT)


def _silu(x):
    h = 0.5 * x
    return h + h * jnp.tanh(h)


def _in_proj_kernel(x_ref, w_ref, cq_ref, saq_ref, sbq_ref, ck_ref, sak_ref, sbk_ref,
                    o_ref, u_ref, xb_ref, acc_ref, *, n_q, n_qk, n_plain, n_main, n_tiles, kv_heads):
    g = pl.program_id(0)
    last = pl.num_programs(0) - 1
    n = g % n_tiles
    cur, prv = g % 2, (g + 1) % 2

    def matmul():
        half = xb_ref.shape[0] // 2
        for c in range(2):
            acc_ref[cur, c * half:(c + 1) * half, :] = jnp.dot(
                xb_ref[c * half:(c + 1) * half, :], w_ref[...], preferred_element_type=F32)

    def cast_rows():
        xb_ref[...] = x_ref[...].astype(BF16)

    def norm_rope(cos_ref, sa_ref, sb_ref, heads):
        for h in range(heads):
            t = acc_ref[prv, :, h * HEAD_DIM:(h + 1) * HEAD_DIM]
            r = lax.rsqrt(jnp.mean(t * t, axis=-1, keepdims=True) + RMS_EPS)
            y = (t * cos_ref[...]
                 + pltpu.roll(t, HEAD_DIM - ROPE_SECTION // 2, 1) * sa_ref[...]
                 + pltpu.roll(t, ROPE_SECTION // 2, 1) * sb_ref[...])
            o_ref[:, h * HEAD_DIM:(h + 1) * HEAD_DIM] = (y * r).astype(o_ref.dtype)

    def rope_q():
        norm_rope(cq_ref, saq_ref, sbq_ref, o_ref.shape[1] // HEAD_DIM)

    def rope_k_and_v():
        norm_rope(ck_ref, sak_ref, sbk_ref, kv_heads)
        o_ref[:, kv_heads * HEAD_DIM:] = acc_ref[prv, :, kv_heads * HEAD_DIM:].astype(o_ref.dtype)

    def plain():
        o_ref[...] = acc_ref[prv].astype(o_ref.dtype)

    def silu():
        o_ref[...] = _silu(acc_ref[prv]).astype(o_ref.dtype)

    def ssm_input():
        u_ref[...] = acc_ref[prv]

    @pl.when(g == 0)
    def _():
        cast_rows()
        matmul()

    @pl.when((n == 0) & (g > 0) & (g < last))
    def _():
        ssm_input()
        cast_rows()
        matmul()

    @pl.when(g == last)
    def _():
        ssm_input()

    bounds = [(0, n_q, rope_q), (n_q, n_qk, rope_k_and_v), (n_qk, n_plain, plain),
              (n_plain, n_main, silu), (n_main, n_tiles, ssm_input)]
    for lo, hi, epilogue in bounds:
        hi = min(hi, n_tiles - 1)

        @pl.when((n > lo) & (n <= hi) & (g < last))
        def _(epilogue=epilogue):
            epilogue()
            matmul()


def _in_proj(x, w, tables, seq_len, widths, bm, bn):
    T, D = x.shape
    att, kv, ssm, mem = widths
    assert bn == 2 * kv and att % bn == 0 and ssm % bn == 0 and mem % bn == 0
    n_q = att // bn
    n_qk = n_q + 1
    n_plain = n_qk + mem // bn
    n_main = n_plain + (att + ssm + mem) // bn
    n_u = ssm // bn
    n_tiles = n_main + n_u
    w_main = n_main * bn
    row_blocks = T // bm
    pos_blocks = seq_len // bm
    steps = row_blocks * n_tiles + 1

    def row(g):
        return jnp.minimum(g // n_tiles, row_blocks - 1)

    def done(g):
        gp = jnp.maximum(g - 1, 0)
        return gp // n_tiles, gp % n_tiles

    tab_spec = pl.BlockSpec((bm, HEAD_DIM), lambda g: (row(g) % pos_blocks, 0))
    kern = functools.partial(_in_proj_kernel, n_q=n_q, n_qk=n_qk, n_plain=n_plain, n_main=n_main,
                             n_tiles=n_tiles, kv_heads=kv // HEAD_DIM)
    return pl.pallas_call(
        kern,
        grid=(steps,),
        in_specs=[pl.BlockSpec((bm, D), lambda g: (row(g), 0)),
                  pl.BlockSpec((D, bn), lambda g: (0, jnp.where(g == steps - 1, n_tiles - 1, g % n_tiles)))]
                 + [tab_spec] * 6,
        out_specs=[pl.BlockSpec((bm, bn), lambda g: (done(g)[0], jnp.minimum(done(g)[1], n_main - 1))),
                   pl.BlockSpec((bm, bn), lambda g: (done(g)[0], jnp.clip(done(g)[1] - n_main, 0, n_u - 1)))],
        out_shape=[jax.ShapeDtypeStruct((T, w_main), BF16),
                   jax.ShapeDtypeStruct((T, ssm), F32)],
        scratch_shapes=[pltpu.VMEM((bm, D), BF16), pltpu.VMEM((2, bm, bn), F32)],
        compiler_params=_params(("arbitrary",)),
        name="in_proj",
    )(x, w, *tables)


def _attn_kernel(q_ref, k_ref, v_ref, g_ref, o_ref, kt_ref, vp_ref, acc_ref, m_ref, *, bk, unroll):
    bq = q_ref.shape[0]
    n_rep = q_ref.shape[1] // HEAD_DIM
    seq = k_ref.shape[0]
    nk = seq // bk

    @pl.when(pl.program_id(2) == 0)
    def _():
        for c in range(seq // bk):
            kt_ref[:, c * bk:(c + 1) * bk] = k_ref[c * bk:(c + 1) * bk, :].T
        vp_ref[:, :HEAD_DIM] = v_ref[...]
        vp_ref[:, HEAD_DIM:] = jnp.ones((seq, HEAD_DIM), vp_ref.dtype)

    acc_ref[...] = jnp.zeros_like(acc_ref)
    m_ref[...] = jnp.full(m_ref.shape, -1e30, F32)

    def body(j, carry):
        off = pl.multiple_of(j * bk, bk)
        kt = kt_ref[:, pl.ds(off, bk)]
        vp = vp_ref[pl.ds(off, bk), :]
        for h in range(n_rep):
            q = q_ref[:, h * HEAD_DIM:(h + 1) * HEAD_DIM]
            s = jnp.dot(q, kt, preferred_element_type=F32)
            m_old = m_ref[h]
            m_new = jnp.maximum(m_old, jnp.max(s, axis=-1, keepdims=True))
            p = jnp.exp2(s - jnp.concatenate([m_new] * (bk // HEAD_DIM), axis=1)).astype(BF16)
            alpha = jnp.exp2(m_old - m_new)
            acc_ref[h] = (jnp.concatenate([alpha, alpha], axis=1) * acc_ref[h]
                          + jnp.dot(p, vp, preferred_element_type=F32))
            m_ref[h] = m_new
        return carry

    lax.fori_loop(0, nk, body, 0, unroll=unroll)
    for h in range(n_rep):
        a = acc_ref[h]
        sl = slice(h * HEAD_DIM, (h + 1) * HEAD_DIM)
        o_ref[:, sl] = (a[:, :HEAD_DIM] / a[:, HEAD_DIM:] * g_ref[:, sl].astype(F32)).astype(o_ref.dtype)


def ATTN_BLOCKS(seq_len):
    bk = _pick(seq_len, 512)
    return _pick(seq_len, 1024), bk, math.gcd(seq_len // bk, 8)


def _attention(main, batch, seq_len, widths, bq, bk, unroll):
    att, kv, ssm, mem = widths
    T = main.shape[0]
    n_rep = att // kv
    wq = n_rep * HEAD_DIM
    k_col = att // HEAD_DIM
    v_col = (att + kv) // HEAD_DIM
    g_col = (att + 2 * kv + mem) // wq
    qb = seq_len // bq
    return pl.pallas_call(
        functools.partial(_attn_kernel, bk=bk, unroll=unroll),
        grid=(batch, N_KV_HEADS, qb),
        in_specs=[pl.BlockSpec((bq, wq), lambda b, g, i: (b * qb + i, g)),
                  pl.BlockSpec((seq_len, HEAD_DIM), lambda b, g, i: (b, k_col + g)),
                  pl.BlockSpec((seq_len, HEAD_DIM), lambda b, g, i: (b, v_col + g)),
                  pl.BlockSpec((bq, wq), lambda b, g, i: (b * qb + i, g_col + g))],
        out_specs=pl.BlockSpec((bq, wq), lambda b, g, i: (b * qb + i, g)),
        out_shape=jax.ShapeDtypeStruct((T, att), BF16),
        scratch_shapes=[pltpu.VMEM((HEAD_DIM, seq_len), BF16),
                        pltpu.VMEM((seq_len, 2 * HEAD_DIM), BF16),
                        pltpu.VMEM((n_rep, bq, 2 * HEAD_DIM), F32),
                        pltpu.VMEM((n_rep, bq, HEAD_DIM), F32)],
        compiler_params=_params(("parallel", "parallel", "arbitrary")),
        name="gqa_attention",
    )(main, main, main, main)


def _matmul_kernel(x_ref, w_ref, o_ref):
    o_ref[...] = jnp.dot(x_ref[...].astype(BF16), w_ref[...],
                         preferred_element_type=F32).astype(o_ref.dtype)


def _matmul(x, w, bn):
    M, K = x.shape
    N = w.shape[1]
    return pl.pallas_call(
        _matmul_kernel,
        grid=(N // bn,),
        in_specs=[pl.BlockSpec((M, K), lambda n: (0, 0)),
                  pl.BlockSpec((K, bn), lambda n: (0, n))],
        out_specs=pl.BlockSpec((M, bn), lambda n: (0, n)),
        out_shape=jax.ShapeDtypeStruct((M, N), BF16),
        compiler_params=_params(("arbitrary",)),
        name="mem_kv_proj",
    )(x, w)


def _mem_attn_kernel(q_ref, k_ref, v_ref, g_ref, o_ref, *, scale):
    hd = q_ref.shape[1] // N_MEM_HEADS
    for h in range(N_MEM_HEADS):
        sl = slice(h * hd, (h + 1) * hd)
        q = (q_ref[:, sl].astype(F32) * scale).astype(BF16)
        s = lax.dot_general(q, k_ref[:, sl], (((1,), (1,)), ((), ())), preferred_element_type=F32)
        p = jnp.exp2(s - jnp.max(s, axis=-1, keepdims=True))
        l = jnp.sum(p, axis=-1, keepdims=True)
        o = jnp.dot(p.astype(BF16), v_ref[:, sl], preferred_element_type=F32)
        o_ref[:, sl] = (o / l * g_ref[:, sl].astype(F32)).astype(o_ref.dtype)


def _mem_attention(main, memkv, batch, seq_len, n_mem, widths, bl):
    att, kv, ssm, mem = widths
    T = main.shape[0]
    q_col = (att + 2 * kv) // mem
    g_col = (att + 2 * kv + mem + att + ssm) // mem
    lb = seq_len // bl
    scale = (mem // N_MEM_HEADS) ** -0.5 * LOG2E
    return pl.pallas_call(
        functools.partial(_mem_attn_kernel, scale=scale),
        grid=(batch, lb),
        in_specs=[pl.BlockSpec((bl, mem), lambda b, i: (b * lb + i, q_col)),
                  pl.BlockSpec((n_mem, mem), lambda b, i: (b, 0)),
                  pl.BlockSpec((n_mem, mem), lambda b, i: (b, 1)),
                  pl.BlockSpec((bl, mem), lambda b, i: (b * lb + i, g_col))],
        out_specs=pl.BlockSpec((bl, mem), lambda b, i: (b * lb + i, 0)),
        out_shape=jax.ShapeDtypeStruct((T, mem), BF16),
        compiler_params=_params(("parallel", "arbitrary")),
        name="mem_attention",
    )(main, memkv, memkv, main)


def _gelu_tanh(y):
    return 0.5 * y * (1.0 + jnp.tanh(math.sqrt(2.0 / math.pi) * (y + 0.044715 * (y * y * y))))


def _s5_kernel(*refs, reverse, tc):
    if reverse:
        u_ref, yin_ref, bt_ref, ct_ref, wt_ref, et_ref, a_ref, y_ref, bm_ref, cm_ref, uc_ref, s_ref, st_ref = refs
    else:
        (u_ref, kcat_ref, bt_ref, ct_ref, wt_ref, et_ref, a_ref, y_ref,
         tz_ref, bm_ref, cm_ref, uc_ref, s_ref, st_ref) = refs
    rb = uc_ref.shape[0]
    ns = S5_UNIT_STATES
    ch = S5_UNIT

    @pl.when((pl.program_id(1) == 0) & (pl.program_id(2) == 0))
    def _():
        def scaled(f_ref, tab_ref, out_ref, sign):
            fr, fi = f_ref[0], f_ref[1]
            for j in range(tc):
                wr, wi = tab_ref[0, j:j + 1, :], tab_ref[1, j:j + 1, :]
                out_ref[j * ch:(j + 1) * ch, 0:ns] = (fr * wr - fi * wi).astype(BF16)
                out_ref[j * ch:(j + 1) * ch, ns:2 * ns] = (sign * (fr * wi + fi * wr)).astype(BF16)

        scaled(bt_ref, wt_ref, bm_ref, 1.0)
        scaled(ct_ref, et_ref, cm_ref, -1.0)
        if not reverse:
            for j in range(tc):
                tz_ref[j * ch:(j + 1) * ch, :] = kcat_ref[:, (tc - 1 - j) * ch:(2 * tc - 1 - j) * ch]

    @pl.when(pl.program_id(2) == 0)
    def _():
        st_ref[...] = jnp.zeros_like(st_ref)

    for j in range(tc):
        uc_ref[:, j * ch:(j + 1) * ch] = u_ref[pl.ds(j, rb, stride=tc), :].astype(BF16)
    uc = uc_ref[...]
    s_ref[...] = jnp.dot(uc, bm_ref[...], preferred_element_type=F32)
    ar = a_ref[0:1, :]
    ai = a_ref[1:2, :]

    def body(i, carry):
        xr, xi = carry
        c = (rb - 1 - i) if reverse else i
        sr = s_ref[pl.ds(c, 1), 0:ns]
        si = s_ref[pl.ds(c, 1), ns:2 * ns]
        s_ref[pl.ds(c, 1), 0:ns] = xr
        s_ref[pl.ds(c, 1), ns:2 * ns] = xi
        return ar * xr - ai * xi + sr, ar * xi + ai * xr + si

    xr, xi = lax.fori_loop(0, rb, body, (st_ref[0:1, :], st_ref[1:2, :]), unroll=8)
    st_ref[0:1, :] = xr
    st_ref[1:2, :] = xi

    y = lax.dot_general(s_ref[...].astype(BF16), cm_ref[...], (((1,), (1,)), ((), ())),
                        preferred_element_type=F32)
    if not reverse:
        y = y + jnp.dot(uc, tz_ref[...], preferred_element_type=F32)
    for t in range(tc):
        rows = pl.ds(t, rb, stride=tc)
        yt = y[:, t * ch:(t + 1) * ch]
        if reverse:
            y_ref[rows, :] = _gelu_tanh(yin_ref[rows, :] + yt)
        else:
            y_ref[rows, :] = yt


def _s5_pass(u, y_in, ops, batch, seq_len, reverse, rb):
    T, ssm = u.shape
    tc = S5_CHUNK
    n_units = ssm // S5_UNIT
    tok = rb * tc
    nb = seq_len // tok
    ns = S5_UNIT_STATES
    ch = S5_UNIT

    def row(b, r):
        return b * nb + ((nb - 1 - r) if reverse else r)

    tok_spec = pl.BlockSpec((tok, ch), lambda n, b, r: (row(b, r), n))

    def unit_spec(*shape):
        return pl.BlockSpec((None,) + shape, lambda n, b, r: (n,) + (0,) * len(shape))

    d = "b" if reverse else "f"
    factors = [ops["bt" + d], ops["ct"], ops["wt" + d], ops["et" + d], ops["a" + d]]
    factor_specs = [unit_spec(2, ch, ns), unit_spec(2, ch, ns), unit_spec(2, tc, ns), unit_spec(2, tc, ns),
                    unit_spec(2, ns)]
    op_scratch = [pltpu.VMEM((tc * ch, 2 * ns), BF16), pltpu.VMEM((tc * ch, 2 * ns), BF16)]
    if reverse:
        ins, in_specs = [u, y_in], [tok_spec, tok_spec]
    else:
        ins, in_specs = [u, ops["kcat"]], [tok_spec, unit_spec(ch, (2 * tc - 1) * ch)]
        op_scratch = [pltpu.VMEM((tc * ch, tc * ch), BF16)] + op_scratch
    return pl.pallas_call(
        functools.partial(_s5_kernel, reverse=reverse, tc=tc),
        grid=(n_units, batch, nb),
        in_specs=in_specs + factor_specs,
        out_specs=tok_spec,
        out_shape=jax.ShapeDtypeStruct((T, ssm), F32),
        scratch_shapes=op_scratch + [pltpu.VMEM((rb, tc * ch), BF16),
                                     pltpu.VMEM((rb, 2 * ns), F32),
                                     pltpu.VMEM((2, ns), F32)],
        compiler_params=_params(("parallel", "arbitrary", "arbitrary")),
        name="s5_backward" if reverse else "s5_forward",
    )(*ins, *factors)


def _s5_operators(lam_re, lam_im, log_step, b_re, b_im, c_re, c_im, d_skip):
    hi = lax.Precision.HIGHEST
    tc = S5_CHUNK
    G = lam_re.shape[1]
    P, C = SSM_STATE, SSM_GROUP
    gu = S5_UNIT // C
    nu = G // gu
    ns, ch = gu * P, S5_UNIT
    lr = lam_re.astype(F32)
    li = lam_im.astype(F32)
    dt = jnp.exp(log_step.astype(F32))[..., None]
    mag = jnp.exp(lr * dt)
    ar = mag * jnp.cos(li * dt)
    ai = mag * jnp.sin(li * dt)
    den = lr * lr + li * li
    nr = ar - 1.0
    fr = (nr * lr + ai * li) / den
    fi = (ai * lr - nr * li) / den
    br = b_re.astype(F32)
    bi = b_im.astype(F32)
    bbar_r = fr[..., None] * br - fi[..., None] * bi
    bbar_i = fr[..., None] * bi + fi[..., None] * br
    e = jnp.arange(tc + 1, dtype=F32)[None, :, None, None]
    pmag = jnp.exp(lr[:, None] * dt[:, None] * e)
    pang = li[:, None] * dt[:, None] * e
    pr = pmag * jnp.cos(pang)
    pi = pmag * jnp.sin(pang)
    cr = c_re.astype(F32)
    ci = c_im.astype(F32)
    eye = jnp.eye(gu, dtype=F32)

    def unit_cs(m):
        return jnp.einsum('ngcp,gh->ngchp', m.reshape(nu, gu, C, P), eye).reshape(nu, ch, ns)

    def unit_table(t):
        return jnp.transpose(t.reshape(t.shape[0], nu, ns), (1, 0, 2))

    ops = {"ct": jnp.stack([unit_cs(cr), unit_cs(ci)], axis=1)}
    lag = []
    for d, name in ((0, "f"), (1, "b")):
        ops["bt" + name] = jnp.stack([unit_cs(jnp.swapaxes(bbar_r[d], 1, 2)),
                                      unit_cs(jnp.swapaxes(bbar_i[d], 1, 2))], axis=1)
        ex = (tc - 1 - jnp.arange(tc)) if d == 0 else jnp.arange(tc)
        ops["wt" + name] = jnp.stack([unit_table(pr[d][ex]), unit_table(pi[d][ex])], axis=1)
        ex = (jnp.arange(tc) + 1) if d == 0 else (tc - jnp.arange(tc))
        ops["et" + name] = jnp.stack([unit_table(pr[d][ex]), unit_table(pi[d][ex])], axis=1)
        ops["a" + name] = jnp.stack([pr[d][tc].reshape(nu, ns), pi[d][tc].reshape(nu, ns)], axis=1)
        pgr = pr[d][:tc][..., None]
        pgi = pi[d][:tc][..., None]
        ba_r = bbar_r[d][None] * pgr - bbar_i[d][None] * pgi
        ba_i = bbar_r[d][None] * pgi + bbar_i[d][None] * pgr
        lag.append(jnp.einsum('tgpd,gcp->tgdc', ba_r, cr, precision=hi)
                   - jnp.einsum('tgpd,gcp->tgdc', ba_i, ci, precision=hi))
    kf, kb = lag
    k0 = kf[0] + kb[0] + d_skip.astype(F32).reshape(G, C)[:, :, None] * jnp.eye(C, dtype=F32)[None]
    kall = jnp.concatenate([kb[1:][::-1], k0[None], kf[1:]], axis=0)
    kcat = jnp.einsum('lngdc,gh->ngdlhc', kall.reshape(2 * tc - 1, nu, gu, C, C), eye)
    ops["kcat"] = kcat.reshape(nu, ch, (2 * tc - 1) * ch).astype(BF16)
    return ops


def _glu_kernel(y_ref, w_ref, b_ref, g_ref, o_ref):
    y = y_ref[...]
    z = jnp.dot(y.astype(BF16), w_ref[...], preferred_element_type=F32) + b_ref[...]
    o_ref[...] = (y * jax.nn.sigmoid(z) * g_ref[...].astype(F32)).astype(o_ref.dtype)


def _glu(y, w, b, main, widths, bm):
    att, kv, ssm, mem = widths
    T = y.shape[0]
    g_col = (att + 2 * kv + mem + att) // ssm
    return pl.pallas_call(
        _glu_kernel,
        grid=(T // bm,),
        in_specs=[pl.BlockSpec((bm, ssm), lambda m: (m, 0)),
                  pl.BlockSpec((ssm, ssm), lambda m: (0, 0)),
                  pl.BlockSpec((1, ssm), lambda m: (0, 0)),
                  pl.BlockSpec((bm, ssm), lambda m: (m, g_col))],
        out_specs=pl.BlockSpec((bm, ssm), lambda m: (m, 0)),
        out_shape=jax.ShapeDtypeStruct((T, ssm), BF16),
        compiler_params=_params(("parallel",)),
        name="glu",
    )(y, w, b, main)


def _out_kernel(att_ref, ssm_ref, mem_ref, wa_ref, ws_ref, wm_ref, x_ref, g_ref, b_ref,
                o_ref, *, alpha, n_tiles):
    n = pl.program_id(1)
    bm, d = o_ref.shape
    bn = wa_ref.shape[1]

    for j in range(n_tiles):
        @pl.when(n == j)
        def _(j=j):
            h = (jnp.dot(att_ref[...], wa_ref[...], preferred_element_type=F32)
                 + jnp.dot(ssm_ref[...], ws_ref[...], preferred_element_type=F32)
                 + jnp.dot(mem_ref[...], wm_ref[...], preferred_element_type=F32))
            o_ref[:, j * bn:(j + 1) * bn] = alpha * x_ref[...] + h

    @pl.when(n == n_tiles - 1)
    def _():
        cols = [slice(c, c + LN_CHUNK) for c in range(0, d, LN_CHUNK)]
        tot = jnp.zeros((bm, 1), F32)
        for sl in cols:
            tot = tot + jnp.sum(o_ref[:, sl], axis=-1, keepdims=True)
        mu = tot / d
        sq = jnp.zeros((bm, 1), F32)
        for sl in cols:
            hc = o_ref[:, sl] - mu
            sq = sq + jnp.sum(hc * hc, axis=-1, keepdims=True)
        r = lax.rsqrt(sq / d + LN_EPS)
        for sl in cols:
            o_ref[:, sl] = (o_ref[:, sl] - mu) * r * g_ref[:, sl] + b_ref[:, sl]


def _out_proj(att_o, ssm_o, mem_o, w_out, x, ln_g, ln_b, alpha, bm, bn):
    T, D = x.shape
    att, ssm, mem = att_o.shape[1], ssm_o.shape[1], mem_o.shape[1]
    return pl.pallas_call(
        functools.partial(_out_kernel, alpha=alpha, n_tiles=D // bn),
        grid=(T // bm, D // bn),
        in_specs=[pl.BlockSpec((bm, att), lambda m, n: (m, 0)),
                  pl.BlockSpec((bm, ssm), lambda m, n: (m, 0)),
                  pl.BlockSpec((bm, mem), lambda m, n: (m, 0)),
                  pl.BlockSpec((att, bn), lambda m, n: (0, n)),
                  pl.BlockSpec((ssm, bn), lambda m, n: (att // ssm, n)),
                  pl.BlockSpec((mem, bn), lambda m, n: ((att + ssm) // mem, n)),
                  pl.BlockSpec((bm, bn), lambda m, n: (m, n)),
                  pl.BlockSpec((1, D), lambda m, n: (0, 0)),
                  pl.BlockSpec((1, D), lambda m, n: (0, 0))],
        out_specs=pl.BlockSpec((bm, D), lambda m, n: (m, 0)),
        out_shape=jax.ShapeDtypeStruct((T, D), F32),
        compiler_params=_params(("parallel", "arbitrary")),
        name="out_proj",
    )(att_o, ssm_o, mem_o, w_out, w_out, w_out, x, ln_g, ln_b)


def _rope_tables(seq_len, gain, scale):
    rows = seq_len // GRID_W
    row = jnp.broadcast_to(jnp.arange(rows, dtype=F32)[:, None], (rows, GRID_W)).reshape(seq_len)
    col = jnp.broadcast_to(jnp.arange(GRID_W, dtype=F32)[None, :], (rows, GRID_W)).reshape(seq_len)
    inv = ROPE_THETA ** (-jnp.arange(0, ROPE_SECTION, 2, dtype=F32) / ROPE_SECTION)
    ang_r = row[:, None] * inv[None, :]
    ang_c = col[:, None] * inv[None, :]
    ang = jnp.concatenate([ang_r, ang_r, ang_c, ang_c], axis=-1)
    cos, sin = jnp.cos(ang), jnp.sin(ang)
    half = ROPE_SECTION // 2
    first = (jnp.arange(HEAD_DIM) % ROPE_SECTION) < half
    g = gain.astype(F32) * scale
    c_t = cos * g[None, :]
    sa_t = jnp.where(first[None, :], -sin * jnp.roll(g, -half)[None, :], 0.0)
    sb_t = jnp.where(first[None, :], 0.0, sin * jnp.roll(g, half)[None, :])
    return c_t, sa_t, sb_t


def _pick(n, pref):
    b = min(n, pref)
    while n % b:
        b //= 2
    return b


IN_BN = 2 * N_KV_HEADS * HEAD_DIM
OUT_BN = 1024
LN_CHUNK = 512


def INPROJ_BLOCKS(seq_len):
    return _pick(seq_len, 512), IN_BN


def OUTPROJ_BLOCKS(n_tokens):
    return _pick(n_tokens, 512), OUT_BN


def _layer(x3, mem3, w_in_p, q_tabs_fn, widths, ops, w_glu, b_glu, w_mem_kv, w_out, ln_g, ln_b, alpha):
    B, L, D = x3.shape
    att, kv, ssm, mem = widths
    x = x3.reshape(B * L, D)
    n_mem = mem3.shape[1]
    tables = q_tabs_fn(L)
    main, u = _in_proj(x, w_in_p, tables, L, widths, *INPROJ_BLOCKS(L))
    att_o = _attention(main, B, L, widths, *ATTN_BLOCKS(L))
    memkv = _matmul(mem3.reshape(B * n_mem, D), w_mem_kv, 512)
    mem_o = _mem_attention(main, memkv, B, L, n_mem, widths, _pick(L, 512))
    rb = _pick(L // S5_CHUNK, S5_ROW_BLOCK)
    y_f = _s5_pass(u, None, ops, B, L, False, rb)
    y = _s5_pass(u, y_f, ops, B, L, True, rb)
    ssm_o = _glu(y, w_glu, b_glu, main, widths, _pick(B * L, 512))
    out = _out_proj(att_o, ssm_o, mem_o, w_out, x, ln_g, ln_b, alpha, *OUTPROJ_BLOCKS(B * L))
    return out.reshape(B, L, D)


def kernel(x_prompt, x_sample, mem_prompt, mem_sample, w_in, q_norm_g, k_norm_g, ssm_lam_re, ssm_lam_im, ssm_log_step, ssm_b_re, ssm_b_im, ssm_c_re, ssm_c_im, ssm_d, w_glu, b_glu, w_mem_kv, w_out, ln_g, ln_b):
    depth, d_model, in_width = w_in.shape
    att = d_model // 2
    kv = N_KV_HEADS * HEAD_DIM
    ssm = d_model // 4
    mem = d_model - att - ssm
    widths = (att, kv, ssm, mem)
    alpha = (2 * depth) ** 0.25
    o = [0, att, att + kv, att + 2 * kv, 2 * att + 2 * kv, 2 * att + 2 * kv + ssm,
         2 * att + 2 * kv + 2 * ssm, 2 * att + 2 * kv + 2 * ssm + mem, in_width]
    seg = lambda i: (o[i], o[i + 1])
    order = [seg(0), seg(1), seg(2), seg(6), seg(3), seg(5), seg(7), seg(4)]

    y_p, y_s = x_prompt, x_sample
    for l in range(depth):
        w_in_p = jnp.concatenate([w_in[l][:, a:b] for a, b in order], axis=1).astype(BF16)
        att_scale = HEAD_DIM ** -0.5 * LOG2E

        def tabs(L, l=l):
            return (_rope_tables(L, q_norm_g[l], att_scale) + _rope_tables(L, k_norm_g[l], 1.0))

        ops = _s5_operators(ssm_lam_re[l], ssm_lam_im[l], ssm_log_step[l], ssm_b_re[l], ssm_b_im[l],
                            ssm_c_re[l], ssm_c_im[l], ssm_d[l])
        args = (w_in_p, tabs, widths, ops, w_glu[l].astype(BF16), b_glu[l].astype(F32)[None, :],
                w_mem_kv[l].astype(BF16), w_out[l].astype(BF16),
                ln_g[l].astype(F32)[None, :], ln_b[l].astype(F32)[None, :], alpha)
        y_p = _layer(y_p, mem_prompt, *args)
        y_s = _layer(y_s, mem_sample, *args)
    return (y_p, y_s)
```

```python
import functools
import math

import jax
import jax.numpy as jnp
from jax import lax
from jax.experimental import pallas as pl
from jax.experimental.pallas import tpu as pltpu

F32 = jnp.float32
BF16 = jnp.bfloat16

HEAD_DIM = 128
N_KV_HEADS = 4
GRID_W = 64
ROPE_SECTION = HEAD_DIM // 2
ROPE_THETA = 10000.0
RMS_EPS = 1e-6
LN_EPS = 1e-5
SSM_GROUP = 16
SSM_STATE = 64
N_MEM_HEADS = 4
LOG2E = 1.4426950408889634

V7X_LANES = 128
V7X_VMEM_BYTES = 64 * 1024 * 1024
VMEM_LIMIT = V7X_VMEM_BYTES - 6 * 1024 * 1024

S5_CHUNK = 16
S5_UNIT = V7X_LANES
S5_UNIT_STATES = (S5_UNIT // SSM_GROUP) * SSM_STATE
S5_ROW_BLOCK = 256


def _params(sem):
    return pltpu.CompilerParams(dimension_semantics=sem, vmem_limit_bytes=VMEM_LIMIT)


def _silu(x):
    h = 0.5 * x
    return h + h * jnp.tanh(h)


def _in_proj_kernel(x_ref, w_ref, cq_ref, saq_ref, sbq_ref, ck_ref, sak_ref, sbk_ref,
                    o_ref, u_ref, xb_ref, acc_ref, *, n_q, n_qk, n_plain, n_main, n_tiles, kv_heads):
    g = pl.program_id(0)
    last = pl.num_programs(0) - 1
    n = g % n_tiles
    cur, prv = g % 2, (g + 1) % 2

    def matmul():
        acc_ref[cur] = jnp.dot(xb_ref[...], w_ref[...], preferred_element_type=F32)

    def cast_rows():
        xb_ref[...] = x_ref[...].astype(BF16)

    def norm_rope(cos_ref, sa_ref, sb_ref, heads):
        for h in range(heads):
            t = acc_ref[prv, :, h * HEAD_DIM:(h + 1) * HEAD_DIM]
            r = lax.rsqrt(jnp.mean(t * t, axis=-1, keepdims=True) + RMS_EPS)
            y = (t * cos_ref[...]
                 + pltpu.roll(t, HEAD_DIM - ROPE_SECTION // 2, 1) * sa_ref[...]
                 + pltpu.roll(t, ROPE_SECTION // 2, 1) * sb_ref[...])
            o_ref[:, h * HEAD_DIM:(h + 1) * HEAD_DIM] = (y * r).astype(o_ref.dtype)

    def rope_q():
        norm_rope(cq_ref, saq_ref, sbq_ref, o_ref.shape[1] // HEAD_DIM)

    def rope_k_and_v():
        norm_rope(ck_ref, sak_ref, sbk_ref, kv_heads)
        o_ref[:, kv_heads * HEAD_DIM:] = acc_ref[prv, :, kv_heads * HEAD_DIM:].astype(o_ref.dtype)

    def plain():
        o_ref[...] = acc_ref[prv].astype(o_ref.dtype)

    def silu():
        o_ref[...] = _silu(acc_ref[prv]).astype(o_ref.dtype)

    def ssm_input():
        u_ref[...] = acc_ref[prv]

    @pl.when(g == 0)
    def _():
        cast_rows()
        matmul()

    @pl.when((n == 0) & (g > 0) & (g < last))
    def _():
        ssm_input()
        cast_rows()
        matmul()

    @pl.when(g == last)
    def _():
        ssm_input()

    bounds = [(0, n_q, rope_q), (n_q, n_qk, rope_k_and_v), (n_qk, n_plain, plain),
              (n_plain, n_main, silu), (n_main, n_tiles, ssm_input)]
    for lo, hi, epilogue in bounds:
        hi = min(hi, n_tiles - 1)

        @pl.when((n > lo) & (n <= hi) & (g < last))
        def _(epilogue=epilogue):
            epilogue()
            matmul()


def _in_proj(x, w, tables, seq_len, widths, bm, bn):
    T, D = x.shape
    att, kv, ssm, mem = widths
    assert bn == 2 * kv and att % bn == 0 and ssm % bn == 0 and mem % bn == 0
    n_q = att // bn
    n_qk = n_q + 1
    n_plain = n_qk + mem // bn
    n_main = n_plain + (att + ssm + mem) // bn
    n_u = ssm // bn
    n_tiles = n_main + n_u
    w_main = n_main * bn
    row_blocks = T // bm
    pos_blocks = seq_len // bm
    steps = row_blocks * n_tiles + 1

    def row(g):
        return jnp.minimum(g // n_tiles, row_blocks - 1)

    def done(g):
        gp = jnp.maximum(g - 1, 0)
        return gp // n_tiles, gp % n_tiles

    tab_spec = pl.BlockSpec((bm, HEAD_DIM), lambda g: (row(g) % pos_blocks, 0))
    kern = functools.partial(_in_proj_kernel, n_q=n_q, n_qk=n_qk, n_plain=n_plain, n_main=n_main,
                             n_tiles=n_tiles, kv_heads=kv // HEAD_DIM)
    return pl.pallas_call(
        kern,
        grid=(steps,),
        in_specs=[pl.BlockSpec((bm, D), lambda g: (row(g), 0)),
                  pl.BlockSpec((D, bn), lambda g: (0, jnp.where(g == steps - 1, n_tiles - 1, g % n_tiles)))]
                 + [tab_spec] * 6,
        out_specs=[pl.BlockSpec((bm, bn), lambda g: (done(g)[0], jnp.minimum(done(g)[1], n_main - 1))),
                   pl.BlockSpec((bm, bn), lambda g: (done(g)[0], jnp.clip(done(g)[1] - n_main, 0, n_u - 1)))],
        out_shape=[jax.ShapeDtypeStruct((T, w_main), BF16),
                   jax.ShapeDtypeStruct((T, ssm), F32)],
        scratch_shapes=[pltpu.VMEM((bm, D), BF16), pltpu.VMEM((2, bm, bn), F32)],
        compiler_params=_params(("arbitrary",)),
        name="in_proj",
    )(x, w, *tables)


def _attn_kernel(q_ref, k_ref, v_ref, g_ref, o_ref, kt_ref, vp_ref, acc_ref, m_ref, *, bk, unroll):
    bq = q_ref.shape[0]
    n_rep = q_ref.shape[1] // HEAD_DIM
    seq = k_ref.shape[0]
    nk = seq // bk

    @pl.when(pl.program_id(2) == 0)
    def _():
        for c in range(seq // bk):
            kt_ref[:, c * bk:(c + 1) * bk] = k_ref[c * bk:(c + 1) * bk, :].T
        vp_ref[:, :HEAD_DIM] = v_ref[...]
        vp_ref[:, HEAD_DIM:] = jnp.ones((seq, HEAD_DIM), vp_ref.dtype)

    acc_ref[...] = jnp.zeros_like(acc_ref)
    m_ref[...] = jnp.full(m_ref.shape, -1e30, F32)

    def body(j, carry):
        off = pl.multiple_of(j * bk, bk)
        kt = kt_ref[:, pl.ds(off, bk)]
        vp = vp_ref[pl.ds(off, bk), :]
        for h in range(n_rep):
            q = q_ref[:, h * HEAD_DIM:(h + 1) * HEAD_DIM]
            s = jnp.dot(q, kt, preferred_element_type=F32)
            m_old = m_ref[h]
            m_new = jnp.maximum(m_old, jnp.max(s, axis=-1, keepdims=True))
            p = jnp.exp2(s - jnp.concatenate([m_new] * (bk // HEAD_DIM), axis=1)).astype(BF16)
            alpha = jnp.exp2(m_old - m_new)
            acc_ref[h] = (jnp.concatenate([alpha, alpha], axis=1) * acc_ref[h]
                          + jnp.dot(p, vp, preferred_element_type=F32))
            m_ref[h] = m_new
        return carry

    lax.fori_loop(0, nk, body, 0, unroll=unroll)
    for h in range(n_rep):
        a = acc_ref[h]
        sl = slice(h * HEAD_DIM, (h + 1) * HEAD_DIM)
        o_ref[:, sl] = (a[:, :HEAD_DIM] / a[:, HEAD_DIM:] * g_ref[:, sl].astype(F32)).astype(o_ref.dtype)


def ATTN_BLOCKS(seq_len):
    bk = _pick(seq_len, 512)
    return _pick(seq_len, 1024), bk, math.gcd(seq_len // bk, 8)


def _attention(main, batch, seq_len, widths, bq, bk, unroll):
    att, kv, ssm, mem = widths
    T = main.shape[0]
    n_rep = att // kv
    wq = n_rep * HEAD_DIM
    k_col = att // HEAD_DIM
    v_col = (att + kv) // HEAD_DIM
    g_col = (att + 2 * kv + mem) // wq
    qb = seq_len // bq
    return pl.pallas_call(
        functools.partial(_attn_kernel, bk=bk, unroll=unroll),
        grid=(batch, N_KV_HEADS, qb),
        in_specs=[pl.BlockSpec((bq, wq), lambda b, g, i: (b * qb + i, g)),
                  pl.BlockSpec((seq_len, HEAD_DIM), lambda b, g, i: (b, k_col + g)),
                  pl.BlockSpec((seq_len, HEAD_DIM), lambda b, g, i: (b, v_col + g)),
                  pl.BlockSpec((bq, wq), lambda b, g, i: (b * qb + i, g_col + g))],
        out_specs=pl.BlockSpec((bq, wq), lambda b, g, i: (b * qb + i, g)),
        out_shape=jax.ShapeDtypeStruct((T, att), BF16),
        scratch_shapes=[pltpu.VMEM((HEAD_DIM, seq_len), BF16),
                        pltpu.VMEM((seq_len, 2 * HEAD_DIM), BF16),
                        pltpu.VMEM((n_rep, bq, 2 * HEAD_DIM), F32),
                        pltpu.VMEM((n_rep, bq, HEAD_DIM), F32)],
        compiler_params=_params(("parallel", "parallel", "arbitrary")),
        name="gqa_attention",
    )(main, main, main, main)


def _matmul_kernel(x_ref, w_ref, o_ref):
    o_ref[...] = jnp.dot(x_ref[...].astype(BF16), w_ref[...],
                         preferred_element_type=F32).astype(o_ref.dtype)


def _matmul(x, w, bn):
    M, K = x.shape
    N = w.shape[1]
    return pl.pallas_call(
        _matmul_kernel,
        grid=(N // bn,),
        in_specs=[pl.BlockSpec((M, K), lambda n: (0, 0)),
                  pl.BlockSpec((K, bn), lambda n: (0, n))],
        out_specs=pl.BlockSpec((M, bn), lambda n: (0, n)),
        out_shape=jax.ShapeDtypeStruct((M, N), BF16),
        compiler_params=_params(("arbitrary",)),
        name="mem_kv_proj",
    )(x, w)


def _mem_attn_kernel(q_ref, k_ref, v_ref, g_ref, o_ref, *, scale):
    hd = q_ref.shape[1] // N_MEM_HEADS
    for h in range(N_MEM_HEADS):
        sl = slice(h * hd, (h + 1) * hd)
        q = (q_ref[:, sl].astype(F32) * scale).astype(BF16)
        s = lax.dot_general(q, k_ref[:, sl], (((1,), (1,)), ((), ())), preferred_element_type=F32)
        p = jnp.exp2(s - jnp.max(s, axis=-1, keepdims=True))
        l = jnp.sum(p, axis=-1, keepdims=True)
        o = jnp.dot(p.astype(BF16), v_ref[:, sl], preferred_element_type=F32)
        o_ref[:, sl] = (o / l * g_ref[:, sl].astype(F32)).astype(o_ref.dtype)


def _mem_attention(main, memkv, batch, seq_len, n_mem, widths, bl):
    att, kv, ssm, mem = widths
    T = main.shape[0]
    q_col = (att + 2 * kv) // mem
    g_col = (att + 2 * kv + mem + att + ssm) // mem
    lb = seq_len // bl
    scale = (mem // N_MEM_HEADS) ** -0.5 * LOG2E
    return pl.pallas_call(
        functools.partial(_mem_attn_kernel, scale=scale),
        grid=(batch, lb),
        in_specs=[pl.BlockSpec((bl, mem), lambda b, i: (b * lb + i, q_col)),
                  pl.BlockSpec((n_mem, mem), lambda b, i: (b, 0)),
                  pl.BlockSpec((n_mem, mem), lambda b, i: (b, 1)),
                  pl.BlockSpec((bl, mem), lambda b, i: (b * lb + i, g_col))],
        out_specs=pl.BlockSpec((bl, mem), lambda b, i: (b * lb + i, 0)),
        out_shape=jax.ShapeDtypeStruct((T, mem), BF16),
        compiler_params=_params(("parallel", "arbitrary")),
        name="mem_attention",
    )(main, memkv, memkv, main)


def _gelu_tanh(y):
    return 0.5 * y * (1.0 + jnp.tanh(math.sqrt(2.0 / math.pi) * (y + 0.044715 * (y * y * y))))


def _s5_kernel(*refs, reverse, tc):
    if reverse:
        u_ref, yin_ref, bt_ref, ct_ref, wt_ref, et_ref, a_ref, y_ref, bm_ref, cm_ref, uc_ref, s_ref, st_ref = refs
    else:
        (u_ref, kcat_ref, bt_ref, ct_ref, wt_ref, et_ref, a_ref, y_ref,
         tz_ref, bm_ref, cm_ref, uc_ref, s_ref, st_ref) = refs
    rb = uc_ref.shape[0]
    ns = S5_UNIT_STATES
    ch = S5_UNIT

    @pl.when((pl.program_id(1) == 0) & (pl.program_id(2) == 0))
    def _():
        def scaled(f_ref, tab_ref, out_ref, sign):
            fr, fi = f_ref[0], f_ref[1]
            for j in range(tc):
                wr, wi = tab_ref[0, j:j + 1, :], tab_ref[1, j:j + 1, :]
                out_ref[j * ch:(j + 1) * ch, 0:ns] = (fr * wr - fi * wi).astype(BF16)
                out_ref[j * ch:(j + 1) * ch, ns:2 * ns] = (sign * (fr * wi + fi * wr)).astype(BF16)

        scaled(bt_ref, wt_ref, bm_ref, 1.0)
        scaled(ct_ref, et_ref, cm_ref, -1.0)
        if not reverse:
            for j in range(tc):
                tz_ref[j * ch:(j + 1) * ch, :] = kcat_ref[:, (tc - 1 - j) * ch:(2 * tc - 1 - j) * ch]

    @pl.when(pl.program_id(2) == 0)
    def _():
        st_ref[...] = jnp.zeros_like(st_ref)

    for j in range(tc):
        uc_ref[:, j * ch:(j + 1) * ch] = u_ref[pl.ds(j, rb, stride=tc), :].astype(BF16)
    uc = uc_ref[...]
    s_ref[...] = jnp.dot(uc, bm_ref[...], preferred_element_type=F32)
    ar = a_ref[0:1, :]
    ai = a_ref[1:2, :]

    def body(i, carry):
        xr, xi = carry
        c = (rb - 1 - i) if reverse else i
        sr = s_ref[pl.ds(c, 1), 0:ns]
        si = s_ref[pl.ds(c, 1), ns:2 * ns]
        s_ref[pl.ds(c, 1), 0:ns] = xr
        s_ref[pl.ds(c, 1), ns:2 * ns] = xi
        return ar * xr - ai * xi + sr, ar * xi + ai * xr + si

    xr, xi = lax.fori_loop(0, rb, body, (st_ref[0:1, :], st_ref[1:2, :]), unroll=8)
    st_ref[0:1, :] = xr
    st_ref[1:2, :] = xi

    y = lax.dot_general(s_ref[...].astype(BF16), cm_ref[...], (((1,), (1,)), ((), ())),
                        preferred_element_type=F32)
    if not reverse:
        y = y + jnp.dot(uc, tz_ref[...], preferred_element_type=F32)
    for t in range(tc):
        rows = pl.ds(t, rb, stride=tc)
        yt = y[:, t * ch:(t + 1) * ch]
        if reverse:
            y_ref[rows, :] = _gelu_tanh(yin_ref[rows, :] + yt)
        else:
            y_ref[rows, :] = yt


def _s5_pass(u, y_in, ops, batch, seq_len, reverse, rb):
    T, ssm = u.shape
    tc = S5_CHUNK
    n_units = ssm // S5_UNIT
    tok = rb * tc
    nb = seq_len // tok
    ns = S5_UNIT_STATES
    ch = S5_UNIT

    def row(b, r):
        return b * nb + ((nb - 1 - r) if reverse else r)

    tok_spec = pl.BlockSpec((tok, ch), lambda n, b, r: (row(b, r), n))

    def unit_spec(*shape):
        return pl.BlockSpec((None,) + shape, lambda n, b, r: (n,) + (0,) * len(shape))

    d = "b" if reverse else "f"
    factors = [ops["bt" + d], ops["ct"], ops["wt" + d], ops["et" + d], ops["a" + d]]
    factor_specs = [unit_spec(2, ch, ns), unit_spec(2, ch, ns), unit_spec(2, tc, ns), unit_spec(2, tc, ns),
                    unit_spec(2, ns)]
    op_scratch = [pltpu.VMEM((tc * ch, 2 * ns), BF16), pltpu.VMEM((tc * ch, 2 * ns), BF16)]
    if reverse:
        ins, in_specs = [u, y_in], [tok_spec, tok_spec]
    else:
        ins, in_specs = [u, ops["kcat"]], [tok_spec, unit_spec(ch, (2 * tc - 1) * ch)]
        op_scratch = [pltpu.VMEM((tc * ch, tc * ch), BF16)] + op_scratch
    return pl.pallas_call(
        functools.partial(_s5_kernel, reverse=reverse, tc=tc),
        grid=(n_units, batch, nb),
        in_specs=in_specs + factor_specs,
        out_specs=tok_spec,
        out_shape=jax.ShapeDtypeStruct((T, ssm), F32),
        scratch_shapes=op_scratch + [pltpu.VMEM((rb, tc * ch), BF16),
                                     pltpu.VMEM((rb, 2 * ns), F32),
                                     pltpu.VMEM((2, ns), F32)],
        compiler_params=_params(("parallel", "arbitrary", "arbitrary")),
        name="s5_backward" if reverse else "s5_forward",
    )(*ins, *factors)


def _s5_operators(lam_re, lam_im, log_step, b_re, b_im, c_re, c_im, d_skip):
    hi = lax.Precision.HIGHEST
    tc = S5_CHUNK
    G = lam_re.shape[1]
    P, C = SSM_STATE, SSM_GROUP
    gu = S5_UNIT // C
    nu = G // gu
    ns, ch = gu * P, S5_UNIT
    lr = lam_re.astype(F32)
    li = lam_im.astype(F32)
    dt = jnp.exp(log_step.astype(F32))[..., None]
    mag = jnp.exp(lr * dt)
    ar = mag * jnp.cos(li * dt)
    ai = mag * jnp.sin(li * dt)
    den = lr * lr + li * li
    nr = ar - 1.0
    fr = (nr * lr + ai * li) / den
    fi = (ai * lr - nr * li) / den
    br = b_re.astype(F32)
    bi = b_im.astype(F32)
    bbar_r = fr[..., None] * br - fi[..., None] * bi
    bbar_i = fr[..., None] * bi + fi[..., None] * br
    e = jnp.arange(tc + 1, dtype=F32)[None, :, None, None]
    pmag = jnp.exp(lr[:, None] * dt[:, None] * e)
    pang = li[:, None] * dt[:, None] * e
    pr = pmag * jnp.cos(pang)
    pi = pmag * jnp.sin(pang)
    cr = c_re.astype(F32)
    ci = c_im.astype(F32)
    eye = jnp.eye(gu, dtype=F32)

    def unit_cs(m):
        return jnp.einsum('ngcp,gh->ngchp', m.reshape(nu, gu, C, P), eye).reshape(nu, ch, ns)

    def unit_table(t):
        return jnp.transpose(t.reshape(t.shape[0], nu, ns), (1, 0, 2))

    ops = {"ct": jnp.stack([unit_cs(cr), unit_cs(ci)], axis=1)}
    lag = []
    for d, name in ((0, "f"), (1, "b")):
        ops["bt" + name] = jnp.stack([unit_cs(jnp.swapaxes(bbar_r[d], 1, 2)),
                                      unit_cs(jnp.swapaxes(bbar_i[d], 1, 2))], axis=1)
        ex = (tc - 1 - jnp.arange(tc)) if d == 0 else jnp.arange(tc)
        ops["wt" + name] = jnp.stack([unit_table(pr[d][ex]), unit_table(pi[d][ex])], axis=1)
        ex = (jnp.arange(tc) + 1) if d == 0 else (tc - jnp.arange(tc))
        ops["et" + name] = jnp.stack([unit_table(pr[d][ex]), unit_table(pi[d][ex])], axis=1)
        ops["a" + name] = jnp.stack([pr[d][tc].reshape(nu, ns), pi[d][tc].reshape(nu, ns)], axis=1)
        pgr = pr[d][:tc][..., None]
        pgi = pi[d][:tc][..., None]
        ba_r = bbar_r[d][None] * pgr - bbar_i[d][None] * pgi
        ba_i = bbar_r[d][None] * pgi + bbar_i[d][None] * pgr
        lag.append(jnp.einsum('tgpd,gcp->tgdc', ba_r, cr, precision=hi)
                   - jnp.einsum('tgpd,gcp->tgdc', ba_i, ci, precision=hi))
    kf, kb = lag
    k0 = kf[0] + kb[0] + d_skip.astype(F32).reshape(G, C)[:, :, None] * jnp.eye(C, dtype=F32)[None]
    kall = jnp.concatenate([kb[1:][::-1], k0[None], kf[1:]], axis=0)
    kcat = jnp.einsum('lngdc,gh->ngdlhc', kall.reshape(2 * tc - 1, nu, gu, C, C), eye)
    ops["kcat"] = kcat.reshape(nu, ch, (2 * tc - 1) * ch).astype(BF16)
    return ops


def _glu_kernel(y_ref, w_ref, b_ref, g_ref, o_ref):
    y = y_ref[...]
    z = jnp.dot(y.astype(BF16), w_ref[...], preferred_element_type=F32) + b_ref[...]
    o_ref[...] = (y * jax.nn.sigmoid(z) * g_ref[...].astype(F32)).astype(o_ref.dtype)


def _glu(y, w, b, main, widths, bm):
    att, kv, ssm, mem = widths
    T = y.shape[0]
    g_col = (att + 2 * kv + mem + att) // ssm
    return pl.pallas_call(
        _glu_kernel,
        grid=(T // bm,),
        in_specs=[pl.BlockSpec((bm, ssm), lambda m: (m, 0)),
                  pl.BlockSpec((ssm, ssm), lambda m: (0, 0)),
                  pl.BlockSpec((1, ssm), lambda m: (0, 0)),
                  pl.BlockSpec((bm, ssm), lambda m: (m, g_col))],
        out_specs=pl.BlockSpec((bm, ssm), lambda m: (m, 0)),
        out_shape=jax.ShapeDtypeStruct((T, ssm), BF16),
        compiler_params=_params(("parallel",)),
        name="glu",
    )(y, w, b, main)


def _out_kernel(att_ref, ssm_ref, mem_ref, wa_ref, ws_ref, wm_ref, x_ref, g_ref, b_ref,
                o_ref, *, alpha, n_tiles):
    n = pl.program_id(1)
    bm, d = o_ref.shape
    bn = wa_ref.shape[1]

    for j in range(n_tiles):
        @pl.when(n == j)
        def _(j=j):
            h = (jnp.dot(att_ref[...], wa_ref[...], preferred_element_type=F32)
                 + jnp.dot(ssm_ref[...], ws_ref[...], preferred_element_type=F32)
                 + jnp.dot(mem_ref[...], wm_ref[...], preferred_element_type=F32))
            o_ref[:, j * bn:(j + 1) * bn] = alpha * x_ref[...] + h

    @pl.when(n == n_tiles - 1)
    def _():
        cols = [slice(c, c + LN_CHUNK) for c in range(0, d, LN_CHUNK)]
        tot = jnp.zeros((bm, 1), F32)
        for sl in cols:
            tot = tot + jnp.sum(o_ref[:, sl], axis=-1, keepdims=True)
        mu = tot / d
        sq = jnp.zeros((bm, 1), F32)
        for sl in cols:
            hc = o_ref[:, sl] - mu
            sq = sq + jnp.sum(hc * hc, axis=-1, keepdims=True)
        r = lax.rsqrt(sq / d + LN_EPS)
        for sl in cols:
            o_ref[:, sl] = (o_ref[:, sl] - mu) * r * g_ref[:, sl] + b_ref[:, sl]


def _out_proj(att_o, ssm_o, mem_o, w_out, x, ln_g, ln_b, alpha, bm, bn):
    T, D = x.shape
    att, ssm, mem = att_o.shape[1], ssm_o.shape[1], mem_o.shape[1]
    return pl.pallas_call(
        functools.partial(_out_kernel, alpha=alpha, n_tiles=D // bn),
        grid=(T // bm, D // bn),
        in_specs=[pl.BlockSpec((bm, att), lambda m, n: (m, 0)),
                  pl.BlockSpec((bm, ssm), lambda m, n: (m, 0)),
                  pl.BlockSpec((bm, mem), lambda m, n: (m, 0)),
                  pl.BlockSpec((att, bn), lambda m, n: (0, n)),
                  pl.BlockSpec((ssm, bn), lambda m, n: (att // ssm, n)),
                  pl.BlockSpec((mem, bn), lambda m, n: ((att + ssm) // mem, n)),
                  pl.BlockSpec((bm, bn), lambda m, n: (m, n)),
                  pl.BlockSpec((1, D), lambda m, n: (0, 0)),
                  pl.BlockSpec((1, D), lambda m, n: (0, 0))],
        out_specs=pl.BlockSpec((bm, D), lambda m, n: (m, 0)),
        out_shape=jax.ShapeDtypeStruct((T, D), F32),
        compiler_params=_params(("parallel", "arbitrary")),
        name="out_proj",
    )(att_o, ssm_o, mem_o, w_out, w_out, w_out, x, ln_g, ln_b)


def _rope_tables(seq_len, gain, scale):
    rows = seq_len // GRID_W
    row = jnp.broadcast_to(jnp.arange(rows, dtype=F32)[:, None], (rows, GRID_W)).reshape(seq_len)
    col = jnp.broadcast_to(jnp.arange(GRID_W, dtype=F32)[None, :], (rows, GRID_W)).reshape(seq_len)
    inv = ROPE_THETA ** (-jnp.arange(0, ROPE_SECTION, 2, dtype=F32) / ROPE_SECTION)
    ang_r = row[:, None] * inv[None, :]
    ang_c = col[:, None] * inv[None, :]
    ang = jnp.concatenate([ang_r, ang_r, ang_c, ang_c], axis=-1)
    cos, sin = jnp.cos(ang), jnp.sin(ang)
    half = ROPE_SECTION // 2
    first = (jnp.arange(HEAD_DIM) % ROPE_SECTION) < half
    g = gain.astype(F32) * scale
    c_t = cos * g[None, :]
    sa_t = jnp.where(first[None, :], -sin * jnp.roll(g, -half)[None, :], 0.0)
    sb_t = jnp.where(first[None, :], 0.0, sin * jnp.roll(g, half)[None, :])
    return c_t, sa_t, sb_t


def _pick(n, pref):
    b = min(n, pref)
    while n % b:
        b //= 2
    return b


IN_BN = 2 * N_KV_HEADS * HEAD_DIM
OUT_BN = 1024
LN_CHUNK = 512


def INPROJ_BLOCKS(seq_len):
    return _pick(seq_len, 512), IN_BN


def OUTPROJ_BLOCKS(n_tokens):
    return _pick(n_tokens, 512), OUT_BN


def _layer(x3, mem3, w_in_p, q_tabs_fn, widths, ops, w_glu, b_glu, w_mem_kv, w_out, ln_g, ln_b, alpha):
    B, L, D = x3.shape
    att, kv, ssm, mem = widths
    x = x3.reshape(B * L, D)
    n_mem = mem3.shape[1]
    tables = q_tabs_fn(L)
    main, u = _in_proj(x, w_in_p, tables, L, widths, *INPROJ_BLOCKS(L))
    att_o = _attention(main, B, L, widths, *ATTN_BLOCKS(L))
    memkv = _matmul(mem3.reshape(B * n_mem, D), w_mem_kv, 512)
    mem_o = _mem_attention(main, memkv, B, L, n_mem, widths, _pick(L, 512))
    rb = _pick(L // S5_CHUNK, S5_ROW_BLOCK)
    y_f = _s5_pass(u, None, ops, B, L, False, rb)
    y = _s5_pass(u, y_f, ops, B, L, True, rb)
    ssm_o = _glu(y, w_glu, b_glu, main, widths, _pick(B * L, 512))
    out = _out_proj(att_o, ssm_o, mem_o, w_out, x, ln_g, ln_b, alpha, *OUTPROJ_BLOCKS(B * L))
    return out.reshape(B, L, D)


def kernel(x_prompt, x_sample, mem_prompt, mem_sample, w_in, q_norm_g, k_norm_g, ssm_lam_re, ssm_lam_im, ssm_log_step, ssm_b_re, ssm_b_im, ssm_c_re, ssm_c_im, ssm_d, w_glu, b_glu, w_mem_kv, w_out, ln_g, ln_b):
    depth, d_model, in_width = w_in.shape
    att = d_model // 2
    kv = N_KV_HEADS * HEAD_DIM
    ssm = d_model // 4
    mem = d_model - att - ssm
    widths = (att, kv, ssm, mem)
    alpha = (2 * depth) ** 0.25
    o = [0, att, att + kv, att + 2 * kv, 2 * att + 2 * kv, 2 * att + 2 * kv + ssm,
         2 * att + 2 * kv + 2 * ssm, 2 * att + 2 * kv + 2 * ssm + mem, in_width]
    seg = lambda i: (o[i], o[i + 1])
    order = [seg(0), seg(1), seg(2), seg(6), seg(3), seg(5), seg(7), seg(4)]

    y_p, y_s = x_prompt, x_sample
    for l in range(depth):
        w_in_p = jnp.concatenate([w_in[l][:, a:b] for a, b in order], axis=1).astype(BF16)
        att_scale = HEAD_DIM ** -0.5 * LOG2E

        def tabs(L, l=l):
            return (_rope_tables(L, q_norm_g[l], att_scale) + _rope_tables(L, k_norm_g[l], 1.0))

        ops = _s5_operators(ssm_lam_re[l], ssm_lam_im[l], ssm_log_step[l], ssm_b_re[l], ssm_b_im[l],
                            ssm_c_re[l], ssm_c_im[l], ssm_d[l])
        args = (w_in_p, tabs, widths, ops, w_glu[l].astype(BF16), b_glu[l].astype(F32)[None, :],
                w_mem_kv[l].astype(BF16), w_out[l].astype(BF16),
                ln_g[l].astype(F32)[None, :], ln_b[l].astype(F32)[None, :], alpha)
        y_p = _layer(y_p, mem_prompt, *args)
        y_s = _layer(y_s, mem_sample, *args)
    return (y_p, y_s)
```
